```python
import math
import jax
import jax.numpy as jnp
from jax import lax
import numpy as np

D_MODEL = 1024
BATCH = 1
SEQ = 16384
DEPTH = 2
DEC_BATCH = 128
DEC_SEQ = 4
PAST_LEN = 16384
PAGE_SIZE = 128

N_EVEN = (DEPTH + 1) // 2
N_ODD = DEPTH // 2

MLA_HEADS = 8
MLA_Q_LORA = 192
MLA_KV_LORA = 128
MLA_NOPE = 64
MLA_ROPE = 32
MLA_V = 64
MLA_WIDTH = MLA_HEADS * MLA_V
ROPE_BASE = 10000.0

DIFF_HEADS = 4
DIFF_KV_HEADS = 2
DIFF_HEAD_DIM = 64
DIFF_WIDTH = DIFF_HEADS * 2 * DIFF_HEAD_DIM
SUBLN_EPS = 1e-5

MOBA_HEADS = 8
MOBA_KV_HEADS = 2
MOBA_HEAD_DIM = 128
MOBA_WIDTH = MOBA_HEADS * MOBA_HEAD_DIM
MOBA_BLOCK = 256
MOBA_TOPK = 3

NUM_BUCKETS = 32
REL_MAX_DISTANCE = 128
REL_MAX_EXACT = NUM_BUCKETS // 2
N_BIAS_HEADS = 8

QUERY_BLOCK = 128
NORM_EPS = 1e-6
NEG_INF = -1e30

EVEN_SPLITS = (MLA_Q_LORA, MLA_KV_LORA, MLA_ROPE, MLA_WIDTH,
               DIFF_HEADS * 2 * DIFF_HEAD_DIM, DIFF_KV_HEADS * 2 * DIFF_HEAD_DIM,
               DIFF_KV_HEADS * 2 * DIFF_HEAD_DIM, DIFF_WIDTH)
EVEN_IN = sum(EVEN_SPLITS)
ODD_SPLITS = (MOBA_HEADS * MOBA_HEAD_DIM, MOBA_KV_HEADS * MOBA_HEAD_DIM,
              MOBA_KV_HEADS * MOBA_HEAD_DIM, MOBA_WIDTH)
ODD_IN = sum(ODD_SPLITS)

kernel_name = 'hybrid_mla_diff_moba_decode_step'


def split_cols(z, sizes):
    return jnp.split(z, [int(c) for c in np.cumsum(sizes)[:-1]], axis=-1)


def rms_norm(x, g, eps=NORM_EPS):
    xf = x.astype(jnp.float32)
    y = xf * lax.rsqrt(jnp.mean(xf * xf, axis=-1, keepdims=True) + eps)
    return (y * g.astype(jnp.float32)).astype(x.dtype)


def rope(x, pos):
    half = x.shape[-1] // 2
    inv = ROPE_BASE ** (-jnp.arange(half, dtype=jnp.float32) / half)
    ang = pos.astype(jnp.float32)[:, None] * inv
    shape = (1, pos.shape[0]) + (1,) * (x.ndim - 3) + (half,)
    cos = jnp.cos(ang).reshape(shape)
    sin = jnp.sin(ang).reshape(shape)
    xf = x.astype(jnp.float32)
    x1, x2 = xf[..., :half], xf[..., half:]
    return jnp.concatenate([x1 * cos - x2 * sin, x2 * cos + x1 * sin], axis=-1).astype(x.dtype)


def rel_bucket(dist):
    n = jnp.maximum(dist, 0)
    scaled = jnp.log(jnp.maximum(n, 1).astype(jnp.float32) / REL_MAX_EXACT) / math.log(REL_MAX_DISTANCE / REL_MAX_EXACT)
    large = jnp.minimum(REL_MAX_EXACT + (scaled * (NUM_BUCKETS - REL_MAX_EXACT)).astype(jnp.int32), NUM_BUCKETS - 1)
    return jnp.where(n < REL_MAX_EXACT, n, large)


def t5_bias(qpos, kpos, table):
    return table[rel_bucket(qpos[:, None] - kpos[None, :])].astype(jnp.float32)


def joint_softmax(*logits):
    sizes = [l.shape[-1] for l in logits]
    p = jax.nn.softmax(jnp.concatenate(logits, axis=-1), axis=-1)
    return jnp.split(p, [int(c) for c in np.cumsum(sizes)[:-1]], axis=-1)


def sweep_query_blocks(block_fn, n_blocks):
    out = lax.map(block_fn, jnp.arange(n_blocks))
    out = jnp.moveaxis(out, 0, 1)
    return out.reshape((out.shape[0], n_blocks * out.shape[2]) + out.shape[3:])


def diff_lambda_value(lp, lambda_init):
    lp = lp.astype(jnp.float32)
    return jnp.exp(jnp.sum(lp[0] * lp[1])) - jnp.exp(jnp.sum(lp[2] * lp[3])) + lambda_init


def even_inputs(h, pos, w_in, q_norm_g, w_uq, kv_norm_g):
    b, s, _ = h.shape
    cq, ckv, kr, g_mla, dq, dk, dv, g_diff = split_cols(h @ w_in, EVEN_SPLITS)
    q = jnp.einsum('bsc,chd->bshd', rms_norm(cq, q_norm_g), w_uq)
    q_nope = q[..., :MLA_NOPE]
    q_rope = rope(q[..., MLA_NOPE:], pos)
    ckv = rms_norm(ckv, kv_norm_g)
    kr = rope(kr, pos)
    dq = dq.reshape(b, s, DIFF_HEADS, 2, DIFF_HEAD_DIM)
    dk = dk.reshape(b, s, DIFF_KV_HEADS, 2 * DIFF_HEAD_DIM)
    dv = dv.reshape(b, s, DIFF_KV_HEADS, 2 * DIFF_HEAD_DIM)
    return q_nope, q_rope, ckv, kr, g_mla, dq, dk, dv, g_diff


def mla_prompt(q_nope, q_rope, ckv, kr, w_uk, w_uv):
    b, s, h, _ = q_nope.shape
    k = jnp.concatenate([jnp.einsum('bsc,chd->bshd', ckv, w_uk),
                         jnp.broadcast_to(kr[:, :, None, :], (b, s, h, MLA_ROPE))], axis=-1)
    q = jnp.concatenate([q_nope, q_rope], axis=-1)
    v = jnp.einsum('bsc,chd->bshd', ckv, w_uv)
    scale = (MLA_NOPE + MLA_ROPE) ** -0.5
    kpos = jnp.arange(s)

    def block(bi):
        start = bi * QUERY_BLOCK
        qb = lax.dynamic_slice_in_dim(q, start, QUERY_BLOCK, axis=1)
        qpos = start + jnp.arange(QUERY_BLOCK)
        sc = jnp.einsum('bqhd,bshd->bhqs', qb, k).astype(jnp.float32) * scale
        sc = jnp.where(kpos[None, :] <= qpos[:, None], sc, NEG_INF)
        p = jax.nn.softmax(sc, axis=-1).astype(v.dtype)
        return jnp.einsum('bhqs,bshd->bqhd', p, v)

    return sweep_query_blocks(block, s // QUERY_BLOCK)


def mla_sample(q_nope, q_rope, ckv_new, kr_new, ckv_past, kr_past, w_uk, w_uv):
    t = q_nope.shape[1]
    scale = (MLA_NOPE + MLA_ROPE) ** -0.5
    q_lat = jnp.einsum('bthd,chd->bhtc', q_nope, w_uk)
    s_past = (jnp.einsum('bhtc,bpc->bhtp', q_lat, ckv_past)
              + jnp.einsum('bthr,bpr->bhtp', q_rope, kr_past)).astype(jnp.float32) * scale
    s_new = (jnp.einsum('bhtc,bjc->bhtj', q_lat, ckv_new)
             + jnp.einsum('bthr,bjr->bhtj', q_rope, kr_new)).astype(jnp.float32) * scale
    s_new = jnp.where(jnp.arange(t)[None, :] <= jnp.arange(t)[:, None], s_new, NEG_INF)
    p_past, p_new = joint_softmax(s_past, s_new)
    o_lat = (jnp.einsum('bhtp,bpc->bthc', p_past.astype(ckv_past.dtype), ckv_past)
             + jnp.einsum('bhtj,bjc->bthc', p_new.astype(ckv_new.dtype), ckv_new))
    return jnp.einsum('bthc,chd->bthd', o_lat, w_uv)


def diff_bias(qpos, kpos, table):
    g = DIFF_HEADS // DIFF_KV_HEADS
    b = t5_bias(qpos, kpos, table).reshape(qpos.shape[0], kpos.shape[0], 2, DIFF_KV_HEADS, g)
    return b.transpose(3, 4, 2, 0, 1)


def diff_prompt(q, k, v, lam, table):
    b, s = q.shape[:2]
    g = DIFF_HEADS // DIFF_KV_HEADS
    k5 = k.reshape(b, s, DIFF_KV_HEADS, 2, DIFF_HEAD_DIM)
    kpos = jnp.arange(s)
    scale = DIFF_HEAD_DIM ** -0.5

    def block(bi):
        start = bi * QUERY_BLOCK
        qb = lax.dynamic_slice_in_dim(q, start, QUERY_BLOCK, axis=1).reshape(b, QUERY_BLOCK, DIFF_KV_HEADS, g, 2, DIFF_HEAD_DIM)
        qpos = start + jnp.arange(QUERY_BLOCK)
        sc = jnp.einsum('bqkgmd,bskmd->bkgmqs', qb, k5).astype(jnp.float32) * scale + diff_bias(qpos, kpos, table)
        sc = jnp.where(kpos[None, :] <= qpos[:, None], sc, NEG_INF)
        p = jax.nn.softmax(sc, axis=-1)
        w = (p[:, :, :, 0] - lam * p[:, :, :, 1]).astype(v.dtype)
        o = jnp.einsum('bkgqs,bskd->bqkgd', w, v)
        return o.reshape(b, QUERY_BLOCK, DIFF_HEADS, 2 * DIFF_HEAD_DIM)

    return sweep_query_blocks(block, s // QUERY_BLOCK)


def diff_sample(q, k_new, v_new, k_past, v_past, lam, table, qpos, kpos_past):
    db, t = q.shape[:2]
    p_len = k_past.shape[1]
    g = DIFF_HEADS // DIFF_KV_HEADS
    scale = DIFF_HEAD_DIM ** -0.5
    q6 = q.reshape(db, t, DIFF_KV_HEADS, g, 2, DIFF_HEAD_DIM)
    kp = k_past.reshape(db, p_len, DIFF_KV_HEADS, 2, DIFF_HEAD_DIM)
    kn = k_new.reshape(db, t, DIFF_KV_HEADS, 2, DIFF_HEAD_DIM)
    s_past = jnp.einsum('btkgmd,bpkmd->bkgmtp', q6, kp).astype(jnp.float32) * scale + diff_bias(qpos, kpos_past, table)
    s_new = jnp.einsum('btkgmd,bjkmd->bkgmtj', q6, kn).astype(jnp.float32) * scale + diff_bias(qpos, qpos, table)
    s_new = jnp.where(jnp.arange(t)[None, :] <= jnp.arange(t)[:, None], s_new, NEG_INF)
    p_past, p_new = joint_softmax(s_past, s_new)
    w_past = (p_past[:, :, :, 0] - lam * p_past[:, :, :, 1]).astype(v_past.dtype)
    w_new = (p_new[:, :, :, 0] - lam * p_new[:, :, :, 1]).astype(v_new.dtype)
    o = jnp.einsum('bkgtp,bpkd->btkgd', w_past, v_past) + jnp.einsum('bkgtj,bjkd->btkgd', w_new, v_new)
    return o.reshape(db, t, DIFF_HEADS, 2 * DIFF_HEAD_DIM)


def even_output(mla_o, diff_o, g_mla, g_diff, subln_g, lambda_init, w_out):
    b, s = mla_o.shape[:2]
    diff_o = rms_norm(diff_o, subln_g, SUBLN_EPS) * (1.0 - lambda_init)
    mix = jnp.concatenate([mla_o.reshape(b, s, MLA_WIDTH) * jax.nn.silu(g_mla),
                           diff_o.reshape(b, s, DIFF_WIDTH) * jax.nn.silu(g_diff)], axis=-1)
    return mix @ w_out


def odd_inputs(h, w_in):
    b, s, _ = h.shape
    q, k, v, g = split_cols(h @ w_in, ODD_SPLITS)
    return (q.reshape(b, s, MOBA_HEADS, MOBA_HEAD_DIM),
            k.reshape(b, s, MOBA_KV_HEADS, MOBA_HEAD_DIM),
            v.reshape(b, s, MOBA_KV_HEADS, MOBA_HEAD_DIM), g)


def odd_output(o, g, w_out):
    b, s = o.shape[:2]
    return (o.reshape(b, s, MOBA_WIDTH) * jax.nn.silu(g)) @ w_out


def moba_prompt(q, k, v, table):
    b, s, h, d = q.shape
    hk = k.shape[2]
    g = h // hk
    s_pad = -(-s // MOBA_BLOCK) * MOBA_BLOCK
    pad = ((0, 0), (0, s_pad - s), (0, 0), (0, 0))
    q, k, v = jnp.pad(q, pad), jnp.pad(k, pad), jnp.pad(v, pad)
    nb = s_pad // MOBA_BLOCK
    kb = k.reshape(b, nb, MOBA_BLOCK, hk, d)
    vb = v.reshape(b, nb, MOBA_BLOCK, hk, d)
    means = jnp.mean(kb, axis=2, dtype=jnp.float32)
    ksel = min(MOBA_TOPK, nb)
    bidx = jnp.arange(b)[:, None, None, None]
    kvh = (jnp.arange(h) // g)[None, None, :, None]
    hidx = jnp.arange(h)[:, None, None]
    scale = d ** -0.5

    def block(ci):
        start = ci * QUERY_BLOCK
        qb = lax.dynamic_slice_in_dim(q, start, QUERY_BLOCK, axis=1)
        qg = qb.reshape(b, QUERY_BLOCK, hk, g, d)
        qpos = start + jnp.arange(QUERY_BLOCK)
        cur = start // MOBA_BLOCK
        gate = jnp.einsum('bqkgd,bnkd->bqkgn', qg.astype(jnp.float32), means).reshape(b, QUERY_BLOCK, h, nb)
        gate = jnp.where(jnp.arange(nb) < cur, gate, NEG_INF)
        _, idx = lax.top_k(gate, ksel)
        ks = kb[bidx, idx, :, kvh, :]
        vs = vb[bidx, idx, :, kvh, :]
        sel_pos = idx[..., None] * MOBA_BLOCK + jnp.arange(MOBA_BLOCK)
        s_sel = (jnp.einsum('bqhd,bqhnjd->bqhnj', qb, ks).astype(jnp.float32) * scale
                 + table[rel_bucket(qpos[:, None, None, None] - sel_pos), hidx].astype(jnp.float32))
        s_sel = jnp.where((idx < cur)[..., None], s_sel, NEG_INF).reshape(b, QUERY_BLOCK, h, ksel * MOBA_BLOCK)
        ko = lax.dynamic_index_in_dim(kb, cur, axis=1, keepdims=False)
        vo = lax.dynamic_index_in_dim(vb, cur, axis=1, keepdims=False)
        own_pos = cur * MOBA_BLOCK + jnp.arange(MOBA_BLOCK)
        s_own = (jnp.einsum('bqkgd,bjkd->bqkgj', qg, ko).reshape(b, QUERY_BLOCK, h, MOBA_BLOCK).astype(jnp.float32) * scale
                 + t5_bias(qpos, own_pos, table).transpose(0, 2, 1))
        s_own = jnp.where((own_pos[None, :] <= qpos[:, None])[:, None, :], s_own, NEG_INF)
        p_sel, p_own = joint_softmax(s_sel, s_own)
        o_sel = jnp.einsum('bqhnj,bqhnjd->bqhd', p_sel.reshape(b, QUERY_BLOCK, h, ksel, MOBA_BLOCK).astype(v.dtype), vs)
        o_own = jnp.einsum('bqkgj,bjkd->bqkgd', p_own.reshape(b, QUERY_BLOCK, hk, g, MOBA_BLOCK).astype(v.dtype), vo)
        return o_sel + o_own.reshape(b, QUERY_BLOCK, h, d)

    return sweep_query_blocks(block, s_pad // QUERY_BLOCK)[:, :s]


def moba_sample(q, k_new, v_new, cache_k, cache_v, j, page_table, table, past_len):
    db, t, h, d = q.shape
    hk = k_new.shape[2]
    g = h // hk
    ppb = MOBA_BLOCK // PAGE_SIZE
    nfp = past_len // MOBA_BLOCK
    cb = nfp * MOBA_BLOCK
    r = past_len - cb
    qpos = past_len + jnp.arange(t)
    qg = q.reshape(db, t, hk, g, d)
    hidx = jnp.arange(h)[:, None]
    scale = d ** -0.5
    logits = []
    if nfp > 0:
        ksel = min(MOBA_TOPK, nfp)
        page_means = jnp.mean(cache_k[j], axis=1, dtype=jnp.float32)
        means = page_means[page_table[:, :nfp * ppb]].reshape(db, nfp, ppb, hk, d).mean(axis=2)
        gate = jnp.einsum('btkgd,bnkd->btkgn', qg.astype(jnp.float32), means).reshape(db, t, h, nfp)
        _, idx = lax.top_k(gate, ksel)
        phys = page_table[jnp.arange(db)[:, None, None, None, None], idx[..., None] * ppb + jnp.arange(ppb)]
        kvh = (jnp.arange(h) // g)[None, None, :, None, None]
        ks = cache_k[j, phys, :, kvh, :].reshape(db, t, h, ksel * MOBA_BLOCK, d)
        vs = cache_v[j, phys, :, kvh, :].reshape(db, t, h, ksel * MOBA_BLOCK, d)
        sel_pos = (idx[..., None] * MOBA_BLOCK + jnp.arange(MOBA_BLOCK)).reshape(db, t, h, ksel * MOBA_BLOCK)
        logits.append(jnp.einsum('bthd,bthnd->bthn', q, ks).astype(jnp.float32) * scale
                      + table[rel_bucket(qpos[:, None, None] - sel_pos), hidx].astype(jnp.float32))
    if r > 0:
        own_pages = page_table[:, cb // PAGE_SIZE: past_len // PAGE_SIZE]
        ko = cache_k[j, own_pages].reshape(db, r, hk, d)
        vo = cache_v[j, own_pages].reshape(db, r, hk, d)
        logits.append(jnp.einsum('btkgd,bjkd->btkgj', qg, ko).reshape(db, t, h, r).astype(jnp.float32) * scale
                      + t5_bias(qpos, cb + jnp.arange(r), table).transpose(0, 2, 1))
    s_new = (jnp.einsum('btkgd,bjkd->btkgj', qg, k_new).reshape(db, t, h, t).astype(jnp.float32) * scale
             + t5_bias(qpos, qpos, table).transpose(0, 2, 1))
    logits.append(jnp.where((jnp.arange(t)[None, :] <= jnp.arange(t)[:, None])[:, None, :], s_new, NEG_INF))
    probs = iter(joint_softmax(*logits))
    outs = []
    if nfp > 0:
        outs.append(jnp.einsum('bthn,bthnd->bthd', next(probs).astype(vs.dtype), vs))
    if r > 0:
        p = next(probs).reshape(db, t, hk, g, r).astype(vo.dtype)
        outs.append(jnp.einsum('btkgj,bjkd->btkgd', p, vo).reshape(db, t, h, d))
    p = next(probs).reshape(db, t, hk, g, t).astype(v_new.dtype)
    outs.append(jnp.einsum('btkgj,bjkd->btkgd', p, v_new).reshape(db, t, h, d))
    return sum(outs)


def setup_inputs(seed: int = 0) -> dict:
    key = jax.random.key(seed)
    ks = jax.random.split(key, 32)
    f32 = jnp.float32
    n_pages = PAST_LEN // PAGE_SIZE
    n_used = DEC_BATCH * n_pages
    n_pool = n_used + max(1, n_used // 4)

    def nrm(k, shape, scale=1.0):
        return jax.random.normal(k, shape, f32) * scale

    def gain(k, shape):
        return 1.0 + 0.02 * jax.random.normal(k, shape, f32)

    page_table = jax.random.permutation(ks[0], n_pool)[:n_used].reshape(DEC_BATCH, n_pages).astype(jnp.int32)
    return {
        'x_prompt': nrm(ks[1], (BATCH, SEQ, D_MODEL)),
        'x_sample': nrm(ks[2], (DEC_BATCH, DEC_SEQ, D_MODEL)),
        'cache_mla_ckv': nrm(ks[3], (N_EVEN, n_pool, PAGE_SIZE, MLA_KV_LORA)),
        'cache_mla_krope': nrm(ks[4], (N_EVEN, n_pool, PAGE_SIZE, MLA_ROPE)),
        'cache_diff_k': nrm(ks[5], (N_EVEN, n_pool, PAGE_SIZE, DIFF_KV_HEADS, 2 * DIFF_HEAD_DIM)),
        'cache_diff_v': nrm(ks[6], (N_EVEN, n_pool, PAGE_SIZE, DIFF_KV_HEADS, 2 * DIFF_HEAD_DIM)),
        'cache_moba_k': nrm(ks[7], (N_ODD, n_pool, PAGE_SIZE, MOBA_KV_HEADS, MOBA_HEAD_DIM)),
        'cache_moba_v': nrm(ks[8], (N_ODD, n_pool, PAGE_SIZE, MOBA_KV_HEADS, MOBA_HEAD_DIM)),
        'page_table': page_table,
        'norm_g': gain(ks[9], (DEPTH, D_MODEL)),
        'final_norm_g': gain(ks[10], (D_MODEL,)),
        'rel_bias': nrm(ks[11], (NUM_BUCKETS, N_BIAS_HEADS), 0.2),
        'w_in_even': nrm(ks[12], (N_EVEN, D_MODEL, EVEN_IN), D_MODEL ** -0.5),
        'mla_q_norm_g': gain(ks[13], (N_EVEN, MLA_Q_LORA)),
        'mla_w_uq': nrm(ks[14], (N_EVEN, MLA_Q_LORA, MLA_HEADS, MLA_NOPE + MLA_ROPE), MLA_Q_LORA ** -0.5),
        'mla_kv_norm_g': gain(ks[15], (N_EVEN, MLA_KV_LORA)),
        'mla_w_uk': nrm(ks[16], (N_EVEN, MLA_KV_LORA, MLA_HEADS, MLA_NOPE), MLA_KV_LORA ** -0.5),
        'mla_w_uv': nrm(ks[17], (N_EVEN, MLA_KV_LORA, MLA_HEADS, MLA_V), MLA_KV_LORA ** -0.5),
        'diff_lambda': nrm(ks[18], (N_EVEN, 4, DIFF_HEAD_DIM), 0.1),
        'diff_subln_g': gain(ks[19], (N_EVEN, 2 * DIFF_HEAD_DIM)),
        'w_out_even': nrm(ks[20], (N_EVEN, MLA_WIDTH + DIFF_WIDTH, D_MODEL), (MLA_WIDTH + DIFF_WIDTH) ** -0.5),
        'w_in_odd': nrm(ks[21], (N_ODD, D_MODEL, ODD_IN), D_MODEL ** -0.5),
        'w_out_odd': nrm(ks[22], (N_ODD, MOBA_WIDTH, D_MODEL), MOBA_WIDTH ** -0.5),
    }


def reference(x_prompt, x_sample, cache_mla_ckv, cache_mla_krope, cache_diff_k, cache_diff_v,
              cache_moba_k, cache_moba_v, page_table, norm_g, final_norm_g, rel_bias,
              w_in_even, mla_q_norm_g, mla_w_uq, mla_kv_norm_g, mla_w_uk, mla_w_uv,
              diff_lambda, diff_subln_g, w_out_even, w_in_odd, w_out_odd):
    s_len = x_prompt.shape[1]
    n_dec, t_len = x_sample.shape[0], x_sample.shape[1]
    past_len = page_table.shape[1] * PAGE_SIZE
    pos_p = jnp.arange(s_len)
    pos_s = past_len + jnp.arange(t_len)
    kpos_past = jnp.arange(past_len)
    xp, xs = x_prompt, x_sample
    mla_ckv_p, mla_kr_p, diff_k_p, diff_v_p, moba_k_p, moba_v_p = [], [], [], [], [], []
    mla_ckv_s, mla_kr_s, diff_k_s, diff_v_s, moba_k_s, moba_v_s = [], [], [], [], [], []
    for layer in range(DEPTH):
        hp = rms_norm(xp, norm_g[layer])
        hs = rms_norm(xs, norm_g[layer])
        if layer % 2 == 0:
            i = layer // 2
            lambda_init = 0.8 - 0.6 * math.exp(-0.3 * layer)
            lam = diff_lambda_value(diff_lambda[i], lambda_init)
            qn, qr, ckv, kr, g_mla, dq, dk, dv, g_diff = even_inputs(hp, pos_p, w_in_even[i], mla_q_norm_g[i], mla_w_uq[i], mla_kv_norm_g[i])
            mla_o = mla_prompt(qn, qr, ckv, kr, mla_w_uk[i], mla_w_uv[i])
            diff_o = diff_prompt(dq, dk, dv, lam, rel_bias)
            xp = xp + even_output(mla_o, diff_o, g_mla, g_diff, diff_subln_g[i], lambda_init, w_out_even[i])
            mla_ckv_p.append(ckv)
            mla_kr_p.append(kr)
            diff_k_p.append(dk)
            diff_v_p.append(dv)
            qn, qr, ckv, kr, g_mla, dq, dk, dv, g_diff = even_inputs(hs, pos_s, w_in_even[i], mla_q_norm_g[i], mla_w_uq[i], mla_kv_norm_g[i])
            ckv_past = cache_mla_ckv[i, page_table].reshape(n_dec, past_len, MLA_KV_LORA)
            kr_past = cache_mla_krope[i, page_table].reshape(n_dec, past_len, MLA_ROPE)
            mla_o = mla_sample(qn, qr, ckv, kr, ckv_past, kr_past, mla_w_uk[i], mla_w_uv[i])
            dk_past = cache_diff_k[i, page_table].reshape(n_dec, past_len, DIFF_KV_HEADS, 2 * DIFF_HEAD_DIM)
            dv_past = cache_diff_v[i, page_table].reshape(n_dec, past_len, DIFF_KV_HEADS, 2 * DIFF_HEAD_DIM)
            diff_o = diff_sample(dq, dk, dv, dk_past, dv_past, lam, rel_bias, pos_s, kpos_past)
            xs = xs + even_output(mla_o, diff_o, g_mla, g_diff, diff_subln_g[i], lambda_init, w_out_even[i])
            mla_ckv_s.append(ckv)
            mla_kr_s.append(kr)
            diff_k_s.append(dk)
            diff_v_s.append(dv)
        else:
            j = layer // 2
            q, k, v, g = odd_inputs(hp, w_in_odd[j])
            xp = xp + odd_output(moba_prompt(q, k, v, rel_bias), g, w_out_odd[j])
            moba_k_p.append(k)
            moba_v_p.append(v)
            q, k, v, g = odd_inputs(hs, w_in_odd[j])
            o = moba_sample(q, k, v, cache_moba_k, cache_moba_v, j, page_table, rel_bias, past_len)
            xs = xs + odd_output(o, g, w_out_odd[j])
            moba_k_s.append(k)
            moba_v_s.append(v)
    y_prompt = rms_norm(xp, final_norm_g)
    y_sample = rms_norm(xs, final_norm_g)
    return (y_prompt, y_sample,
            jnp.stack(mla_ckv_p), jnp.stack(mla_kr_p), jnp.stack(diff_k_p), jnp.stack(diff_v_p),
            jnp.stack(moba_k_p), jnp.stack(moba_v_p),
            jnp.stack(mla_ckv_s), jnp.stack(mla_kr_s), jnp.stack(diff_k_s), jnp.stack(diff_v_s),
            jnp.stack(moba_k_s), jnp.stack(moba_v_s))
```

```python
import functools
import math

import jax
import jax.numpy as jnp
import numpy as np
from jax import lax
from jax.experimental import pallas as pl
from jax.experimental.pallas import tpu as pltpu

F32 = jnp.float32
BF16 = jnp.bfloat16

PAGE_SIZE = 128
MLA_HEADS = 8
MLA_Q_LORA = 192
MLA_KV_LORA = 128
MLA_NOPE = 64
MLA_ROPE = 32
MLA_V = 64
ROPE_BASE = 10000.0
DIFF_HEADS = 4
DIFF_KV_HEADS = 2
DIFF_HEAD_DIM = 64
SUBLN_EPS = 1e-5
MOBA_HEADS = 8
MOBA_KV_HEADS = 2
MOBA_HEAD_DIM = 128
MOBA_BLOCK = 256
MOBA_TOPK = 3
NUM_BUCKETS = 32
REL_MAX_DISTANCE = 128
REL_MAX_EXACT = NUM_BUCKETS // 2
NORM_EPS = 1e-6
NEG_INF = -1e30

LANES = 128
ROW_TILE = 256
ATT_TILE = 256
PAGES_PER_STEP = 8
VMEM_LIMIT = 56 * 1024 * 1024


def _bucket_upper_bounds():
    d = np.arange(0, 4 * REL_MAX_DISTANCE)
    v = np.log(np.maximum(d, 1) / REL_MAX_EXACT) / math.log(REL_MAX_DISTANCE / REL_MAX_EXACT)
    v = v * (NUM_BUCKETS - REL_MAX_EXACT)
    frac = np.abs(v[REL_MAX_EXACT + 1:REL_MAX_DISTANCE] - np.round(v[REL_MAX_EXACT + 1:REL_MAX_DISTANCE]))
    assert frac.min() > 1e-3
    b = np.where(d < REL_MAX_EXACT, d, np.minimum(REL_MAX_EXACT + np.floor(v).astype(np.int64), NUM_BUCKETS - 1))
    hi = [int(d[b == k].max()) if np.any(b == k) else None for k in range(NUM_BUCKETS)]
    far = int(hi[NUM_BUCKETS - 2]) + 1
    return hi, far


BUCKET_HI, FAR_DIST = _bucket_upper_bounds()


def _dot(a, b):
    return jnp.dot(a, b, preferred_element_type=F32)


def _dot_nt(a, b):
    return lax.dot_general(a, b, (((1,), (1,)), ((), ())), preferred_element_type=F32)


def _rms(x, g, eps):
    return x * lax.rsqrt(jnp.mean(x * x, axis=-1, keepdims=True) + eps) * g


def _silu(g):
    return g / (1.0 + jnp.exp(-g))


def _full(shape):
    nd = len(shape)
    return pl.BlockSpec(shape, lambda *_: (0,) * nd)


def _resident(shape):
    nd = len(shape)
    return pl.BlockSpec(shape, lambda *_: (0,) * nd, pipeline_mode=pl.Buffered(1))


def _params(n_axes):
    return pltpu.CompilerParams(dimension_semantics=("arbitrary",) * n_axes, vmem_limit_bytes=VMEM_LIMIT)


def _toeplitz_kernel(tab_ref, out_ref, *, off, causal):
    c = pl.program_id(0)
    shape = out_ref.shape[1:]
    i = lax.broadcasted_iota(jnp.int32, shape, 0)
    j = lax.broadcasted_iota(jnp.int32, shape, 1)
    dist = off + i - j
    acc = jnp.zeros(shape, F32) + tab_ref[NUM_BUCKETS - 1, c]
    for b in range(NUM_BUCKETS - 2, -1, -1):
        if BUCKET_HI[b] is not None:
            acc = jnp.where(dist <= BUCKET_HI[b], tab_ref[b, c], acc)
    acc = acc - tab_ref[NUM_BUCKETS - 1, c]
    if causal:
        acc = jnp.where(j <= i, acc, NEG_INF)
    out_ref[0] = acc


def _toeplitz_bias(tab_pad, rows, cols, off, causal):
    n = tab_pad.shape[1]
    return pl.pallas_call(
        functools.partial(_toeplitz_kernel, off=off, causal=causal),
        out_shape=jax.ShapeDtypeStruct((n, rows, cols), F32),
        grid=(n,),
        in_specs=[pl.BlockSpec(memory_space=pltpu.SMEM)],
        out_specs=pl.BlockSpec((1, rows, cols), lambda c: (c, 0, 0)),
        compiler_params=_params(1),
        name="toeplitz_bias",
    )(tab_pad)


_E_GM, _E_DQ, _E_DK, _E_DV, _E_GD, _E_CKV, _E_CQ, _E_KA, _E_KB, _E_END = (
    0, 512, 1024, 1280, 1536, 2048, 2176, 2432, 2560, 2688)


def _even_in_kernel(x_ref, ng_ref, win_ref, qg_ref, wuq_ref, wukt_ref, kvg_ref, cos_ref, sin_ref,
                    qm_ref, kcat_ref, ckv_ref, kr_ref, gm_ref, dq_ref, dk_ref, dkb_ref, dv_ref, dvb_ref, gd_ref):
    h = _rms(x_ref[...], ng_ref[...], NORM_EPS).astype(BF16)
    z = _dot(h, win_ref[...])
    cos2 = cos_ref[...]
    sin2 = sin_ref[...]
    gm_ref[...] = z[:, _E_GM:_E_DQ]
    gd_ref[...] = z[:, _E_GD:_E_CKV]
    dk = z[:, _E_DK:_E_DV]
    dv = z[:, _E_DV:_E_GD]
    dk_ref[...] = dk
    dkb_ref[...] = dk.astype(BF16)
    dv_ref[...] = dv
    dvb_ref[...] = dv.astype(BF16)
    lane = lax.broadcasted_iota(jnp.int32, (x_ref.shape[0], LANES), 1)
    dscale = DIFF_HEAD_DIM ** -0.5
    group = DIFF_HEADS // DIFF_KV_HEADS
    for kv in range(DIFF_KV_HEADS):
        for g in range(group):
            c0 = _E_DQ + (kv * group + g) * 2 * DIFF_HEAD_DIM
            src = z[:, c0:c0 + 2 * DIFF_HEAD_DIM] * dscale
            dq_ref[kv, g] = jnp.where(lane < DIFF_HEAD_DIM, src, 0.0).astype(BF16)
            dq_ref[kv, group + g] = jnp.where(lane >= DIFF_HEAD_DIM, src, 0.0).astype(BF16)
    ckv = _rms(z[:, _E_CKV:_E_CQ], kvg_ref[...], NORM_EPS)
    kr = z[:, _E_KA:_E_KB] * cos2 + z[:, _E_KB:_E_END] * sin2
    ckv_ref[...] = ckv
    kr_ref[...] = kr[:, :MLA_ROPE]
    kcat_ref[...] = jnp.concatenate([ckv, kr], axis=-1).astype(BF16)
    cq = _rms(z[:, _E_CQ:_E_CQ + MLA_Q_LORA], qg_ref[...], NORM_EPS).astype(BF16)
    qall = _dot(cq, wuq_ref[...])
    qscale = (MLA_NOPE + MLA_ROPE) ** -0.5
    nh = MLA_HEADS * LANES
    for hd in range(MLA_HEADS):
        qn = qall[:, hd * LANES:(hd + 1) * LANES].astype(BF16)
        ql = _dot(qn, wukt_ref[hd])
        qr = (qall[:, nh + hd * LANES:nh + (hd + 1) * LANES] * cos2
              + qall[:, 2 * nh + hd * LANES:2 * nh + (hd + 1) * LANES] * sin2)
        qm_ref[hd] = (jnp.concatenate([ql, qr], axis=-1) * qscale).astype(BF16)


def _even_in(x, ng, win, qg, wuq, wukt, kvg, cos2, sin2):
    n, d = x.shape
    tm = min(ROW_TILE, n)
    row = lambda w: pl.BlockSpec((tm, w), lambda i: (i, 0))
    out_shape = (
        jax.ShapeDtypeStruct((MLA_HEADS, n, 2 * LANES), BF16),
        jax.ShapeDtypeStruct((n, 2 * LANES), BF16),
        jax.ShapeDtypeStruct((n, MLA_KV_LORA), F32),
        jax.ShapeDtypeStruct((n, MLA_ROPE), F32),
        jax.ShapeDtypeStruct((n, 512), F32),
        jax.ShapeDtypeStruct((DIFF_KV_HEADS, 4, n, LANES), BF16),
        jax.ShapeDtypeStruct((n, 256), F32),
        jax.ShapeDtypeStruct((n, 256), BF16),
        jax.ShapeDtypeStruct((n, 256), F32),
        jax.ShapeDtypeStruct((n, 256), BF16),
        jax.ShapeDtypeStruct((n, 512), F32),
    )
    out_specs = (
        pl.BlockSpec((MLA_HEADS, tm, 2 * LANES), lambda i: (0, i, 0)),
        row(2 * LANES), row(MLA_KV_LORA), row(MLA_ROPE), row(512),
        pl.BlockSpec((DIFF_KV_HEADS, 4, tm, LANES), lambda i: (0, 0, i, 0)),
        row(256), row(256), row(256), row(256), row(512),
    )
    return pl.pallas_call(
        _even_in_kernel,
        out_shape=out_shape,
        grid=(n // tm,),
        in_specs=[row(d), _full(ng.shape), _full(win.shape), _full(qg.shape), _full(wuq.shape),
                  _full(wukt.shape), _full(kvg.shape), row(LANES), row(LANES)],
        out_specs=out_specs,
        compiler_params=_params(1),
        name="even_in",
    )(x, ng, win, qg, wuq, wukt, kvg, cos2, sin2)


def _flash_update(s, v, m_ref, l_ref, acc_ref):
    m_prev = m_ref[...]
    m_new = jnp.maximum(m_prev, jnp.max(s, axis=-1, keepdims=True))
    alpha = jnp.exp(m_prev - m_new)
    p = jnp.exp(s - m_new)
    l_ref[...] = alpha * l_ref[...] + jnp.sum(p, axis=-1, keepdims=True)
    acc_ref[...] = alpha * acc_ref[...] + _dot(p.astype(BF16), v)
    m_ref[...] = m_new


def _flash_init(m_ref, l_ref, acc_ref):
    m_ref[...] = jnp.full(m_ref.shape, -jnp.inf, F32)
    l_ref[...] = jnp.zeros(l_ref.shape, F32)
    acc_ref[...] = jnp.zeros(acc_ref.shape, F32)


def _mla_prompt_kernel(q_ref, k_ref, mask_ref, o_ref, m_ref, l_ref, acc_ref):
    i = pl.program_id(0)
    t = ATT_TILE
    q = q_ref[...].reshape(MLA_HEADS * t, 2 * LANES)
    _flash_init(m_ref, l_ref, acc_ref)

    def tile(j, mask):
        k = k_ref[pl.ds(pl.multiple_of(j * t, t), t), :]
        s = _dot_nt(q, k)
        if mask is not None:
            s = (s.reshape(MLA_HEADS, t, t) + mask[None]).reshape(MLA_HEADS * t, t)
        _flash_update(s, k[:, :MLA_KV_LORA], m_ref, l_ref, acc_ref)

    def body(j, carry):
        tile(j, None)
        return carry

    lax.fori_loop(0, i, body, 0)
    tile(i, mask_ref[...])
    o = acc_ref[...] / l_ref[...]
    o_ref[...] = o.reshape(MLA_HEADS, t, MLA_KV_LORA)


def _mla_prompt(qm, kcat, mask):
    s = kcat.shape[0]
    t = ATT_TILE
    rows = MLA_HEADS * t
    return pl.pallas_call(
        _mla_prompt_kernel,
        out_shape=jax.ShapeDtypeStruct((MLA_HEADS, s, MLA_KV_LORA), F32),
        grid=(s // t,),
        in_specs=[pl.BlockSpec((MLA_HEADS, t, 2 * LANES), lambda i: (0, i, 0)),
                  _resident(kcat.shape), _resident(mask.shape)],
        out_specs=pl.BlockSpec((MLA_HEADS, t, MLA_KV_LORA), lambda i: (0, i, 0)),
        scratch_shapes=[pltpu.VMEM((rows, 1), F32), pltpu.VMEM((rows, 1), F32),
                        pltpu.VMEM((rows, MLA_KV_LORA), F32)],
        compiler_params=_params(1),
        name="mla_prompt",
    )(qm, kcat, mask)


def _diff_lambda(lp, lambda_init):
    a = jnp.sum(lp[0:1] * lp[1:2], axis=-1, keepdims=True)
    b = jnp.sum(lp[2:3] * lp[3:4], axis=-1, keepdims=True)
    return jnp.exp(a) - jnp.exp(b) + lambda_init


def _diff_combine(acc, l, lam, rows):
    o = acc / l
    return o[:rows] - lam * o[rows:]


def _diff_prompt_kernel(q_ref, k_ref, v_ref, bias_ref, lp_ref, o_ref, m_ref, l_ref, acc_ref, *, lambda_init):
    i = pl.program_id(0)
    t = ATT_TILE
    rows = 4 * t
    lam = _diff_lambda(lp_ref[...], lambda_init)
    for kv in range(DIFF_KV_HEADS):
        q = q_ref[kv].reshape(rows, LANES)
        c0 = kv * LANES
        _flash_init(m_ref, l_ref, acc_ref)

        def tile(j, bias, c0=c0, q=q):
            r0 = pl.multiple_of(j * t, t)
            s = _dot_nt(q, k_ref[pl.ds(r0, t), c0:c0 + LANES])
            if bias is not None:
                s = s + bias
            _flash_update(s, v_ref[pl.ds(r0, t), c0:c0 + LANES], m_ref, l_ref, acc_ref)

        def body(j, carry, tile=tile):
            tile(j, None)
            return carry

        lax.fori_loop(0, jnp.maximum(i - 1, 0), body, 0)

        @pl.when(i >= 1)
        def _(tile=tile, kv=kv):
            tile(i - 1, bias_ref[kv, 1])

        tile(i, bias_ref[kv, 0])
        o = _diff_combine(acc_ref[...], l_ref[...], lam, 2 * t)
        for g in range(2):
            hd = kv * 2 + g
            o_ref[:, hd * LANES:(hd + 1) * LANES] = o[g * t:(g + 1) * t]


def _diff_prompt(dq, dkb, dvb, bias, lp, lambda_init):
    s = dkb.shape[0]
    t = ATT_TILE
    rows = 4 * t
    return pl.pallas_call(
        functools.partial(_diff_prompt_kernel, lambda_init=lambda_init),
        out_shape=jax.ShapeDtypeStruct((s, DIFF_HEADS * LANES), F32),
        grid=(s // t,),
        in_specs=[pl.BlockSpec((DIFF_KV_HEADS, 4, t, LANES), lambda i: (0, 0, i, 0)),
                  _resident(dkb.shape), _resident(dvb.shape), _resident(bias.shape), _full(lp.shape)],
        out_specs=pl.BlockSpec((t, DIFF_HEADS * LANES), lambda i: (i, 0)),
        scratch_shapes=[pltpu.VMEM((rows, 1), F32), pltpu.VMEM((rows, 1), F32),
                        pltpu.VMEM((rows, LANES), F32)],
        compiler_params=_params(1),
        name="diff_prompt",
    )(dq, dkb, dvb, bias, lp)


_O_Q, _O_K, _O_V, _O_G, _O_END = 0, 1024, 1280, 1536, 2560


def _mid_kernel(x_ref, olat_ref, wuv_ref, diffo_ref, gm_ref, gd_ref, subg_ref, wout_ref, ng_ref, win_ref,
                x1_ref, q_ref, k_ref, kb_ref, v_ref, vb_ref, g_ref, mean_ref, *, sub_scale):
    mla = jnp.concatenate([_dot(olat_ref[hd].astype(BF16), wuv_ref[hd]) for hd in range(MLA_HEADS)], axis=-1)
    dn = jnp.concatenate(
        [_rms(diffo_ref[:, hd * LANES:(hd + 1) * LANES], subg_ref[...], SUBLN_EPS) * sub_scale
         for hd in range(DIFF_HEADS)], axis=-1)
    mix = jnp.concatenate([mla * _silu(gm_ref[...]), dn * _silu(gd_ref[...])], axis=-1).astype(BF16)
    x1 = x_ref[...] + _dot(mix, wout_ref[...])
    x1_ref[...] = x1
    h = _rms(x1, ng_ref[...], NORM_EPS).astype(BF16)
    z = _dot(h, win_ref[...])
    q_ref[...] = z[:, _O_Q:_O_K]
    k = z[:, _O_K:_O_V]
    v = z[:, _O_V:_O_G]
    k_ref[...] = k
    kb_ref[...] = k.astype(BF16)
    v_ref[...] = v
    vb_ref[...] = v.astype(BF16)
    g_ref[...] = z[:, _O_G:_O_END]
    mean_ref[0] = jnp.mean(k, axis=0, keepdims=True)


def _mid(x, olat, wuv, diffo, gm, gd, subg, wout, ng, win, sub_scale):
    n, d = x.shape
    tm = min(ROW_TILE, n)
    row = lambda w: pl.BlockSpec((tm, w), lambda i: (i, 0))
    out_shape = (
        jax.ShapeDtypeStruct((n, d), F32), jax.ShapeDtypeStruct((n, 1024), F32),
        jax.ShapeDtypeStruct((n, 256), F32), jax.ShapeDtypeStruct((n, 256), BF16),
        jax.ShapeDtypeStruct((n, 256), F32), jax.ShapeDtypeStruct((n, 256), BF16),
        jax.ShapeDtypeStruct((n, 1024), F32), jax.ShapeDtypeStruct((n // tm, 1, 256), F32),
    )
    out_specs = (row(d), row(1024), row(256), row(256), row(256), row(256), row(1024),
                 pl.BlockSpec((1, 1, 256), lambda i: (i, 0, 0)))
    return pl.pallas_call(
        functools.partial(_mid_kernel, sub_scale=sub_scale),
        out_shape=out_shape,
        grid=(n // tm,),
        in_specs=[row(d), pl.BlockSpec((MLA_HEADS, tm, MLA_KV_LORA), lambda i: (0, i, 0)), _full(wuv.shape),
                  row(512), row(512), row(512), _full(subg.shape), _full(wout.shape), _full(ng.shape),
                  _full(win.shape)],
        out_specs=out_specs,
        compiler_params=_params(1),
        name="even_out_odd_in",
    )(x, olat, wuv, diffo, gm, gd, subg, wout, ng, win)


def _topk_mask(gate, ksel):
    lane = lax.broadcasted_iota(jnp.int32, gate.shape, 1).astype(F32)
    sel = jnp.zeros(gate.shape, jnp.bool_)
    for _ in range(ksel):
        mx = jnp.max(gate, axis=-1, keepdims=True)
        idx = jnp.min(jnp.where(gate == mx, lane, float(gate.shape[1])), axis=-1, keepdims=True)
        pick = lane == idx
        sel = jnp.logical_or(sel, pick)
        gate = jnp.where(pick, -jnp.inf, gate)
    return sel


def _moba_prompt_kernel(q_ref, mean_ref, k_ref, v_ref, bias_ref, o_ref, m_ref, l_ref, acc_ref):
    i = pl.program_id(0)
    t = ATT_TILE
    nb = mean_ref.shape[0]
    group = MOBA_HEADS // MOBA_KV_HEADS
    rows = group * t
    scale = MOBA_HEAD_DIM ** -0.5
    blk = lax.broadcasted_iota(jnp.int32, (rows, nb), 1)
    for kv in range(MOBA_KV_HEADS):
        c0 = kv * LANES
        qf = jnp.concatenate(
            [q_ref[:, (kv * group + g) * LANES:(kv * group + g + 1) * LANES] for g in range(group)], axis=0)
        gate = lax.dot_general(qf, mean_ref[:, c0:c0 + LANES], (((1,), (1,)), ((), ())),
                               precision=lax.Precision.HIGHEST, preferred_element_type=F32)
        past = blk < i
        sel = jnp.logical_and(_topk_mask(jnp.where(past, gate, NEG_INF), min(MOBA_TOPK, nb)), past)
        selb = jnp.where(sel, 0.0, NEG_INF).astype(BF16)
        q = (qf * scale).astype(BF16)
        _flash_init(m_ref, l_ref, acc_ref)

        def tile(j, bias, selected, c0=c0, q=q, selb=selb):
            r0 = pl.multiple_of(j * t, t)
            s = _dot_nt(q, k_ref[pl.ds(r0, t), c0:c0 + LANES])
            if selected:
                onehot = jnp.where(lax.broadcasted_iota(jnp.int32, (t, nb), 1) == j, 1.0, 0.0).astype(BF16)
                s = s + _dot_nt(selb, onehot)
            if bias is not None:
                s = s + bias
            _flash_update(s, v_ref[pl.ds(r0, t), c0:c0 + LANES], m_ref, l_ref, acc_ref)

        def body(j, carry, tile=tile):
            tile(j, None, True)
            return carry

        lax.fori_loop(0, jnp.maximum(i - 1, 0), body, 0)

        @pl.when(i >= 1)
        def _(tile=tile, kv=kv):
            tile(i - 1, bias_ref[kv, 1], True)

        tile(i, bias_ref[kv, 0], False)
        o = acc_ref[...] / l_ref[...]
        for g in range(group):
            hd = kv * group + g
            o_ref[:, hd * LANES:(hd + 1) * LANES] = o[g * t:(g + 1) * t]


def _moba_prompt(q, means, kb, vb, bias):
    s = kb.shape[0]
    t = ATT_TILE
    rows = (MOBA_HEADS // MOBA_KV_HEADS) * t
    return pl.pallas_call(
        _moba_prompt_kernel,
        out_shape=jax.ShapeDtypeStruct((s, MOBA_HEADS * LANES), F32),
        grid=(s // t,),
        in_specs=[pl.BlockSpec((t, MOBA_HEADS * LANES), lambda i: (i, 0)), _resident(means.shape),
                  _resident(kb.shape), _resident(vb.shape), _resident(bias.shape)],
        out_specs=pl.BlockSpec((t, MOBA_HEADS * LANES), lambda i: (i, 0)),
        scratch_shapes=[pltpu.VMEM((rows, 1), F32), pltpu.VMEM((rows, 1), F32),
                        pltpu.VMEM((rows, LANES), F32)],
        compiler_params=_params(1),
        name="moba_prompt",
    )(q, means, kb, vb, bias)


def _final_kernel(x_ref, o_ref, g_ref, wout_ref, fg_ref, y_ref):
    mix = (o_ref[...] * _silu(g_ref[...])).astype(BF16)
    x2 = x_ref[...] + _dot(mix, wout_ref[...])
    y_ref[...] = _rms(x2, fg_ref[...], NORM_EPS)


def _final(x1, o, g, wout, fg):
    n, d = x1.shape
    tm = min(ROW_TILE, n)
    row = lambda w: pl.BlockSpec((tm, w), lambda i: (i, 0))
    return pl.pallas_call(
        _final_kernel,
        out_shape=jax.ShapeDtypeStruct((n, d), F32),
        grid=(n // tm,),
        in_specs=[row(d), row(1024), row(1024), _full(wout.shape), _full(fg.shape)],
        out_specs=row(d),
        compiler_params=_params(1),
        name="odd_out_final",
    )(x1, o, g, wout, fg)


def _page_specs(block, layer, n):
    def spec(p):
        return pl.BlockSpec((None, None) + block, lambda b, c, pt: (layer, pt[b, c * n + p]) + (0,) * len(block))
    return [spec(p) for p in range(n)]


def _mla_sample_kernel(pt_ref, q_ref, knew_ref, mask_ref, *refs, n):
    ckv_refs, kr_refs = refs[:n], refs[n:2 * n]
    o_ref, m_ref, l_ref, acc_ref = refs[2 * n:]
    c = pl.program_id(1)
    q = q_ref[0]

    @pl.when(c == 0)
    def _():
        _flash_init(m_ref, l_ref, acc_ref)

    for p in range(n):
        ckv = ckv_refs[p][...].astype(BF16)
        kr = kr_refs[p][...].astype(BF16)
        s = _dot_nt(q[:, :MLA_KV_LORA], ckv) + _dot_nt(q[:, MLA_KV_LORA:MLA_KV_LORA + MLA_ROPE], kr)
        _flash_update(s, ckv, m_ref, l_ref, acc_ref)

    @pl.when(c == pl.num_programs(1) - 1)
    def _():
        knew = knew_ref[0]
        s = _dot_nt(q, knew) + mask_ref[...]
        _flash_update(s, knew[:, :MLA_KV_LORA], m_ref, l_ref, acc_ref)
        o_ref[0] = acc_ref[...] / l_ref[...]


def _mla_sample(page_table, q, knew, mask, cache_ckv, cache_kr, layer):
    nseq, rows, _ = q.shape
    n = min(PAGES_PER_STEP, page_table.shape[1])
    steps = page_table.shape[1] // n
    grid_spec = pltpu.PrefetchScalarGridSpec(
        num_scalar_prefetch=1,
        grid=(nseq, steps),
        in_specs=[pl.BlockSpec((1, rows, 2 * LANES), lambda b, c, pt: (b, 0, 0)),
                  pl.BlockSpec((1, LANES, 2 * LANES), lambda b, c, pt: (b, 0, 0)),
                  pl.BlockSpec(mask.shape, lambda b, c, pt: (0, 0))]
        + _page_specs((PAGE_SIZE, MLA_KV_LORA), layer, n) + _page_specs((PAGE_SIZE, MLA_ROPE), layer, n),
        out_specs=pl.BlockSpec((1, rows, MLA_KV_LORA), lambda b, c, pt: (b, 0, 0)),
        scratch_shapes=[pltpu.VMEM((rows, 1), F32), pltpu.VMEM((rows, 1), F32),
                        pltpu.VMEM((rows, MLA_KV_LORA), F32)],
    )
    return pl.pallas_call(
        functools.partial(_mla_sample_kernel, n=n),
        out_shape=jax.ShapeDtypeStruct((nseq, rows, MLA_KV_LORA), F32),
        grid_spec=grid_spec,
        compiler_params=_params(2),
        name="mla_sample",
    )(page_table, q, knew, mask, *([cache_ckv] * n), *([cache_kr] * n))


def _diff_sample_kernel(pt_ref, q_ref, knew_ref, vnew_ref, blast_ref, bnew_ref, lp_ref, *refs, n, lambda_init):
    k_refs, v_refs = refs[:n], refs[n:2 * n]
    o_ref, m_ref, l_ref, acc_ref = refs[2 * n:]
    c = pl.program_id(1)
    last = pl.num_programs(1) - 1

    @pl.when(c == 0)
    def _():
        _flash_init(m_ref, l_ref, acc_ref)

    def tile(kv, k, v, bias):
        s = _dot_nt(q_ref[0, kv], k[:, kv * LANES:(kv + 1) * LANES])
        if bias is not None:
            s = s + bias
        _flash_update(s, v[:, kv * LANES:(kv + 1) * LANES], m_ref.at[kv], l_ref.at[kv], acc_ref.at[kv])

    def pages(with_band):
        for p in range(n):
            k = k_refs[p][...].astype(BF16)
            v = v_refs[p][...].astype(BF16)
            for kv in range(DIFF_KV_HEADS):
                tile(kv, k, v, blast_ref[kv] if (with_band and p == n - 1) else None)

    @pl.when(c != last)
    def _():
        pages(False)

    @pl.when(c == last)
    def _():
        pages(True)
        lam = _diff_lambda(lp_ref[...], lambda_init)
        rows = q_ref.shape[2] // 2
        for kv in range(DIFF_KV_HEADS):
            tile(kv, knew_ref[0], vnew_ref[0], bnew_ref[kv])
            o_ref[0, kv] = _diff_combine(acc_ref[kv], l_ref[kv], lam, rows)


def _diff_sample(page_table, q, knew, vnew, blast, bnew, lp, cache_k, cache_v, layer, lambda_init):
    nseq, _, rows, _ = q.shape
    n = min(PAGES_PER_STEP, page_table.shape[1])
    steps = page_table.shape[1] // n
    seq4 = lambda shape: pl.BlockSpec((1,) + shape, lambda b, c, pt: (b,) + (0,) * len(shape))
    const = lambda a: pl.BlockSpec(a.shape, lambda b, c, pt: (0,) * a.ndim)
    grid_spec = pltpu.PrefetchScalarGridSpec(
        num_scalar_prefetch=1,
        grid=(nseq, steps),
        in_specs=[seq4((DIFF_KV_HEADS, rows, LANES)), seq4((LANES, 256)), seq4((LANES, 256)),
                  const(blast), const(bnew), const(lp)]
        + _page_specs((PAGE_SIZE, 256), layer, n) + _page_specs((PAGE_SIZE, 256), layer, n),
        out_specs=seq4((DIFF_KV_HEADS, rows // 2, LANES)),
        scratch_shapes=[pltpu.VMEM((DIFF_KV_HEADS, rows, 1), F32), pltpu.VMEM((DIFF_KV_HEADS, rows, 1), F32),
                        pltpu.VMEM((DIFF_KV_HEADS, rows, LANES), F32)],
    )
    return pl.pallas_call(
        functools.partial(_diff_sample_kernel, n=n, lambda_init=lambda_init),
        out_shape=jax.ShapeDtypeStruct((nseq, DIFF_KV_HEADS, rows // 2, LANES), F32),
        grid_spec=grid_spec,
        compiler_params=_params(2),
        name="diff_sample",
    )(page_table, q, knew, vnew, blast, bnew, lp, *([cache_k] * n), *([cache_v] * n))


def _moba_sample_kernel(pt_ref, q_ref, knew_ref, vnew_ref, blast_ref, bnew_ref, *refs, n, nblocks):
    k_refs, v_refs = refs[:n], refs[n:2 * n]
    o_ref, gate_ref, mb_ref, lb_ref, ob_ref = refs[2 * n:]
    c = pl.program_id(1)
    last = pl.num_programs(1) - 1
    ppb = MOBA_BLOCK // PAGE_SIZE
    bps = n // ppb
    rows = q_ref.shape[2]
    scale = MOBA_HEAD_DIM ** -0.5
    lane = lax.broadcasted_iota(jnp.int32, (rows, nblocks), 1)

    def put(ref, kv, blk, col):
        ref[kv] = jnp.where(lane == blk, col, ref[kv])

    @pl.when(c == 0)
    def _():
        for ref in (gate_ref, mb_ref, lb_ref):
            ref[...] = jnp.zeros(ref.shape, F32)

    def blocks(with_band):
        for bi in range(bps):
            blk = c * bps + bi
            kf = jnp.concatenate([k_refs[bi * ppb + p][...] for p in range(ppb)], axis=0)
            vb = jnp.concatenate([v_refs[bi * ppb + p][...] for p in range(ppb)], axis=0).astype(BF16)
            kmean = jnp.mean(kf, axis=0, keepdims=True)
            kb = kf.astype(BF16)
            for kv in range(MOBA_KV_HEADS):
                cols = slice(kv * LANES, (kv + 1) * LANES)
                qf = q_ref[0, kv]
                put(gate_ref, kv, blk, jnp.sum(qf * kmean[:, cols], axis=-1, keepdims=True))
                s = _dot_nt((qf * scale).astype(BF16), kb[:, cols])
                if with_band and bi == bps - 1:
                    s = s + blast_ref[kv]
                m = jnp.max(s, axis=-1, keepdims=True)
                p = jnp.exp(s - m)
                put(mb_ref, kv, blk, m)
                put(lb_ref, kv, blk, jnp.sum(p, axis=-1, keepdims=True))
                ob_ref[blk, kv] = _dot(p.astype(BF16), vb[:, cols])

    @pl.when(c != last)
    def _():
        blocks(False)

    @pl.when(c == last)
    def _():
        blocks(True)
        for kv in range(MOBA_KV_HEADS):
            cols = slice(kv * LANES, (kv + 1) * LANES)
            qb = (q_ref[0, kv] * scale).astype(BF16)
            s = _dot_nt(qb, knew_ref[0][:, cols]) + bnew_ref[kv]
            m_new = jnp.max(s, axis=-1, keepdims=True)
            p = jnp.exp(s - m_new)
            l_new = jnp.sum(p, axis=-1, keepdims=True)
            o_new = _dot(p.astype(BF16), vnew_ref[0][:, cols])
            sel = _topk_mask(gate_ref[kv], min(MOBA_TOPK, nblocks))
            m_all = jnp.maximum(jnp.max(jnp.where(sel, mb_ref[kv], -jnp.inf), axis=-1, keepdims=True), m_new)
            w = jnp.where(sel, jnp.exp(mb_ref[kv] - m_all), 0.0)
            w_new = jnp.exp(m_new - m_all)
            den = jnp.sum(w * lb_ref[kv], axis=-1, keepdims=True) + w_new * l_new
            num = w_new * o_new
            for blk in range(nblocks):
                num = num + w[:, blk:blk + 1] * ob_ref[blk, kv]
            o_ref[0, kv] = num / den


def _moba_sample(page_table, q, knew, vnew, blast, bnew, cache_k, cache_v, layer):
    nseq, _, rows, _ = q.shape
    ppb = MOBA_BLOCK // PAGE_SIZE
    nblocks = page_table.shape[1] // ppb
    n = min(PAGES_PER_STEP, page_table.shape[1])
    steps = page_table.shape[1] // n
    seq4 = lambda shape: pl.BlockSpec((1,) + shape, lambda b, c, pt: (b,) + (0,) * len(shape))
    const = lambda a: pl.BlockSpec(a.shape, lambda b, c, pt: (0,) * a.ndim)
    grid_spec = pltpu.PrefetchScalarGridSpec(
        num_scalar_prefetch=1,
        grid=(nseq, steps),
        in_specs=[seq4((MOBA_KV_HEADS, rows, LANES)), seq4((LANES, 256)), seq4((LANES, 256)),
                  const(blast), const(bnew)]
        + _page_specs((PAGE_SIZE, 256), layer, n) + _page_specs((PAGE_SIZE, 256), layer, n),
        out_specs=seq4((MOBA_KV_HEADS, rows, LANES)),
        scratch_shapes=[pltpu.VMEM((MOBA_KV_HEADS, rows, nblocks), F32),
                        pltpu.VMEM((MOBA_KV_HEADS, rows, nblocks), F32),
                        pltpu.VMEM((MOBA_KV_HEADS, rows, nblocks), F32),
                        pltpu.VMEM((nblocks, MOBA_KV_HEADS, rows, LANES), F32)],
    )
    return pl.pallas_call(
        functools.partial(_moba_sample_kernel, n=n, nblocks=nblocks),
        out_shape=jax.ShapeDtypeStruct((nseq, MOBA_KV_HEADS, rows, LANES), F32),
        grid_spec=grid_spec,
        compiler_params=_params(2),
        name="moba_sample",
    )(page_table, q, knew, vnew, blast, bnew, *([cache_k] * n), *([cache_v] * n))


def _rope_tables(pos):
    half = MLA_ROPE // 2
    inv = ROPE_BASE ** (-jnp.arange(half, dtype=F32) / half)
    ang = pos.astype(F32)[:, None] * inv
    pad = jnp.zeros((pos.shape[0], LANES - 2 * half), F32)
    cos = jnp.cos(ang)
    sin = jnp.sin(ang)
    return jnp.concatenate([cos, cos, pad], axis=-1), jnp.concatenate([sin, sin, pad], axis=-1)


def _lane_pad(w, width):
    return jnp.pad(w, ((0, 0), (0, width - w.shape[1])))


def _even_weights(w_in, w_uq, w_uk, w_uv):
    half = MLA_ROPE // 2
    cq, ckv, kr, gm, dq, dk, dv, gd = jnp.split(
        w_in, np.cumsum([MLA_Q_LORA, MLA_KV_LORA, MLA_ROPE, 512, 512, 256, 256])[:], axis=1)
    ka = _lane_pad(kr, LANES)
    kb = _lane_pad(jnp.concatenate([-kr[:, half:], kr[:, :half]], axis=1), LANES)
    win = jnp.concatenate([gm, dq, dk, dv, gd, ckv, _lane_pad(cq, 256), ka, kb], axis=1).astype(BF16)
    nope = jnp.concatenate([_lane_pad(w_uq[:, h, :MLA_NOPE], LANES) for h in range(MLA_HEADS)], axis=1)
    ra = jnp.concatenate([_lane_pad(w_uq[:, h, MLA_NOPE:], LANES) for h in range(MLA_HEADS)], axis=1)
    rb = jnp.concatenate(
        [_lane_pad(jnp.concatenate([-w_uq[:, h, MLA_NOPE + half:], w_uq[:, h, MLA_NOPE:MLA_NOPE + half]], axis=1),
                   LANES) for h in range(MLA_HEADS)], axis=1)
    wuq = jnp.concatenate([nope, ra, rb], axis=1).astype(BF16)
    wukt = jnp.pad(jnp.transpose(w_uk, (1, 2, 0)), ((0, 0), (0, LANES - MLA_NOPE), (0, 0))).astype(BF16)
    wuv = jnp.transpose(w_uv, (1, 0, 2)).astype(BF16)
    return win, wuq, wukt, wuv


def _row_tiles(tab, cols, rows_each):
    return jnp.concatenate([tab[c, :rows_each] for c in cols], axis=0)


def kernel(x_prompt, x_sample, cache_mla_ckv, cache_mla_krope, cache_diff_k, cache_diff_v, cache_moba_k, cache_moba_v, page_table, norm_g, final_norm_g, rel_bias, w_in_even, mla_q_norm_g, mla_w_uq, mla_kv_norm_g, mla_w_uk, mla_w_uv, diff_lambda, diff_subln_g, w_out_even, w_in_odd, w_out_odd):
    _, s_len, d_model = x_prompt.shape
    n_dec, t_len, _ = x_sample.shape
    n_pages = page_table.shape[1]
    past_len = n_pages * PAGE_SIZE
    n_smp = n_dec * t_len
    assert norm_g.shape[0] == 2 and x_prompt.shape[0] == 1
    assert s_len % ATT_TILE == 0 and n_smp % min(ROW_TILE, n_smp) == 0 and past_len % MOBA_BLOCK == 0
    assert ROW_TILE == MOBA_BLOCK == ATT_TILE
    assert n_pages % min(PAGES_PER_STEP, n_pages) == 0 and t_len <= 8
    assert FAR_DIST <= PAGE_SIZE + 1 and FAR_DIST <= ATT_TILE + 1
    dgroup = DIFF_HEADS // DIFF_KV_HEADS
    mgroup = MOBA_HEADS // MOBA_KV_HEADS
    lambda_init = 0.8 - 0.6 * math.exp(-0.3 * 0)

    tab = jnp.pad(rel_bias.astype(F32), ((0, 0), (0, 8)))
    t = ATT_TILE
    band0 = _toeplitz_bias(tab, t, t, 0, True)
    band1 = _toeplitz_bias(tab, t, t, t, False)
    s_last_page = _toeplitz_bias(tab, 8, PAGE_SIZE, PAGE_SIZE, False)
    s_last_block = _toeplitz_bias(tab, 8, MOBA_BLOCK, MOBA_BLOCK, False)
    s_new = _toeplitz_bias(tab, 8, LANES, 0, True)
    zero_col = rel_bias.shape[1]
    diff_cols = [[m * DIFF_HEADS + kv * dgroup + g for m in range(2) for g in range(dgroup)]
                 for kv in range(DIFF_KV_HEADS)]
    moba_cols = [[kv * mgroup + g for g in range(mgroup)] for kv in range(MOBA_KV_HEADS)]
    stack = lambda tb, cols, r: jnp.stack([_row_tiles(tb, cl, r) for cl in cols])
    diff_band = jnp.stack([stack(band0, diff_cols, t), stack(band1, diff_cols, t)], axis=1)
    moba_band = jnp.stack([stack(band0, moba_cols, t), stack(band1, moba_cols, t)], axis=1)
    mla_mask = band0[zero_col]
    mla_new_mask = _row_tiles(s_new, [zero_col] * MLA_HEADS, t_len)
    diff_last = stack(s_last_page, diff_cols, t_len)
    diff_new = stack(s_new, diff_cols, t_len)
    moba_last = stack(s_last_block, moba_cols, t_len)
    moba_new = stack(s_new, moba_cols, t_len)

    win_e, wuq, wukt, wuv = _even_weights(w_in_even[0], mla_w_uq[0], mla_w_uk[0], mla_w_uv[0])
    wout_e = w_out_even[0].astype(BF16)
    win_o = w_in_odd[0].astype(BF16)
    wout_o = w_out_odd[0].astype(BF16)
    ng0, ng1, fg = norm_g[0][None], norm_g[1][None], final_norm_g[None]
    qg, kvg, subg = mla_q_norm_g[0][None], mla_kv_norm_g[0][None], diff_subln_g[0][None]
    lp = diff_lambda[0].astype(F32)

    xp = x_prompt[0]
    xs = x_sample.reshape(n_smp, d_model)
    cos_p, sin_p = _rope_tables(jnp.arange(s_len))
    cos_s, sin_s = _rope_tables(jnp.tile(past_len + jnp.arange(t_len), n_dec))

    pad_new = lambda a: jnp.pad(a.reshape(n_dec, t_len, a.shape[-1]), ((0, 0), (0, LANES - t_len), (0, 0)))

    (qm_p, kcat_p, ckv_p, kr_p, gm_p, dq_p, dk_p, dkb_p, dv_p, dvb_p, gd_p) = _even_in(
        xp, ng0, win_e, qg, wuq, wukt, kvg, cos_p, sin_p)
    olat_p = _mla_prompt(qm_p, kcat_p, mla_mask)
    diffo_p = _diff_prompt(dq_p, dkb_p, dvb_p, diff_band, lp, lambda_init)
    (x1_p, q_p, k_p, kb_p, v_p, vb_p, g_p, means_p) = _mid(
        xp, olat_p, wuv, diffo_p, gm_p, gd_p, subg, wout_e, ng1, win_o, 1.0 - lambda_init)

    (qm_s, kcat_s, ckv_s, kr_s, gm_s, dq_s, dk_s, dkb_s, dv_s, dvb_s, gd_s) = _even_in(
        xs, ng0, win_e, qg, wuq, wukt, kvg, cos_s, sin_s)
    q_mla = qm_s.reshape(MLA_HEADS, n_dec, t_len, 2 * LANES).transpose(1, 0, 2, 3).reshape(
        n_dec, MLA_HEADS * t_len, 2 * LANES)
    olat_s = _mla_sample(page_table, q_mla, pad_new(kcat_s), mla_new_mask, cache_mla_ckv, cache_mla_krope, 0)
    olat_s = olat_s.reshape(n_dec, MLA_HEADS, t_len, MLA_KV_LORA).transpose(1, 0, 2, 3).reshape(
        MLA_HEADS, n_smp, MLA_KV_LORA)
    q_diff = dq_s.reshape(DIFF_KV_HEADS, 4, n_dec, t_len, LANES).transpose(2, 0, 1, 3, 4).reshape(
        n_dec, DIFF_KV_HEADS, 4 * t_len, LANES)
    cdk = cache_diff_k.reshape(cache_diff_k.shape[:3] + (256,))
    cdv = cache_diff_v.reshape(cache_diff_v.shape[:3] + (256,))
    diffo_s = _diff_sample(page_table, q_diff, pad_new(dkb_s), pad_new(dvb_s), diff_last, diff_new, lp,
                           cdk, cdv, 0, lambda_init)
    diffo_s = diffo_s.reshape(n_dec, DIFF_KV_HEADS, dgroup, t_len, LANES).transpose(0, 3, 1, 2, 4).reshape(
        n_smp, DIFF_HEADS * LANES)
    (x1_s, q_s, k_s, kb_s, v_s, vb_s, g_s, _) = _mid(
        xs, olat_s, wuv, diffo_s, gm_s, gd_s, subg, wout_e, ng1, win_o, 1.0 - lambda_init)

    o_p = _moba_prompt(q_p, means_p.reshape(-1, 256), kb_p, vb_p, moba_band)
    y_p = _final(x1_p, o_p, g_p, wout_o, fg)

    q_moba = q_s.reshape(n_dec, t_len, MOBA_KV_HEADS, mgroup, LANES).transpose(0, 2, 3, 1, 4).reshape(
        n_dec, MOBA_KV_HEADS, mgroup * t_len, LANES)
    cmk = cache_moba_k.reshape(cache_moba_k.shape[:3] + (256,))
    cmv = cache_moba_v.reshape(cache_moba_v.shape[:3] + (256,))
    o_s = _moba_sample(page_table, q_moba, pad_new(kb_s), pad_new(vb_s), moba_last, moba_new, cmk, cmv, 0)
    o_s = o_s.reshape(n_dec, MOBA_KV_HEADS, mgroup, t_len, LANES).transpose(0, 3, 1, 2, 4).reshape(
        n_smp, MOBA_HEADS * LANES)
    y_s = _final(x1_s, o_s, g_s, wout_o, fg)

    kv4 = lambda a, b, s: a.reshape(1, b, s, 2, LANES)
    return (y_p[None], y_s.reshape(n_dec, t_len, d_model),
            ckv_p.reshape(1, 1, s_len, MLA_KV_LORA), kr_p.reshape(1, 1, s_len, MLA_ROPE),
            kv4(dk_p, 1, s_len), kv4(dv_p, 1, s_len), kv4(k_p, 1, s_len), kv4(v_p, 1, s_len),
            ckv_s.reshape(1, n_dec, t_len, MLA_KV_LORA), kr_s.reshape(1, n_dec, t_len, MLA_ROPE),
            kv4(dk_s, n_dec, t_len), kv4(dv_s, n_dec, t_len), kv4(k_s, n_dec, t_len), kv4(v_s, n_dec, t_len))
```

```python
import functools
import math

import jax
import jax.numpy as jnp
import numpy as np
from jax import lax
from jax.experimental import pallas as pl
from jax.experimental.pallas import tpu as pltpu

F32 = jnp.float32
BF16 = jnp.bfloat16

PAGE_SIZE = 128
MLA_HEADS = 8
MLA_Q_LORA = 192
MLA_KV_LORA = 128
MLA_NOPE = 64
MLA_ROPE = 32
MLA_V = 64
ROPE_BASE = 10000.0
DIFF_HEADS = 4
DIFF_KV_HEADS = 2
DIFF_HEAD_DIM = 64
SUBLN_EPS = 1e-5
MOBA_HEADS = 8
MOBA_KV_HEADS = 2
MOBA_HEAD_DIM = 128
MOBA_BLOCK = 256
MOBA_TOPK = 3
NUM_BUCKETS = 32
REL_MAX_DISTANCE = 128
REL_MAX_EXACT = NUM_BUCKETS // 2
NORM_EPS = 1e-6
NEG_INF = -1e30
LOG2E = math.log2(math.e)

LANES = 128
ROW_TILE = 256
ATT_TILE = 256
PAGES_PER_STEP = 16
VMEM_LIMIT = 56 * 1024 * 1024

MLA_WIDTH = MLA_HEADS * MLA_V
DIFF_WIDTH = DIFF_HEADS * 2 * DIFF_HEAD_DIM
DIFF_KV_WIDTH = DIFF_KV_HEADS * 2 * DIFF_HEAD_DIM
MOBA_WIDTH = MOBA_HEADS * MOBA_HEAD_DIM
MOBA_KV_WIDTH = MOBA_KV_HEADS * MOBA_HEAD_DIM


def _bucket_upper_bounds():
    d = np.arange(0, 4 * REL_MAX_DISTANCE)
    v = np.log(np.maximum(d, 1) / REL_MAX_EXACT) / math.log(REL_MAX_DISTANCE / REL_MAX_EXACT)
    v = v * (NUM_BUCKETS - REL_MAX_EXACT)
    frac = np.abs(v[REL_MAX_EXACT + 1:REL_MAX_DISTANCE] - np.round(v[REL_MAX_EXACT + 1:REL_MAX_DISTANCE]))
    assert frac.min() > 1e-3
    b = np.where(d < REL_MAX_EXACT, d, np.minimum(REL_MAX_EXACT + np.floor(v).astype(np.int64), NUM_BUCKETS - 1))
    hi = [int(d[b == k].max()) if np.any(b == k) else None for k in range(NUM_BUCKETS)]
    far = int(hi[NUM_BUCKETS - 2]) + 1
    return hi, far


BUCKET_HI, FAR_DIST = _bucket_upper_bounds()


def _dot(a, b):
    return jnp.dot(a, b, preferred_element_type=F32)


def _dot_nt(a, b):
    return lax.dot_general(a, b, (((1,), (1,)), ((), ())), preferred_element_type=F32)


def _rms(x, g, eps):
    return x * lax.rsqrt(jnp.mean(x * x, axis=-1, keepdims=True) + eps) * g


def _silu(g):
    return g / (1.0 + jnp.exp(-g))


def _full(shape):
    nd = len(shape)
    return pl.BlockSpec(shape, lambda *_: (0,) * nd)


def _resident(shape):
    nd = len(shape)
    return pl.BlockSpec(shape, lambda *_: (0,) * nd, pipeline_mode=pl.Buffered(1))


def _params(n_axes):
    return pltpu.CompilerParams(dimension_semantics=("arbitrary",) * n_axes, vmem_limit_bytes=VMEM_LIMIT)


def _toeplitz_kernel(tab_ref, out_ref, *, off, causal, transposed):
    c = pl.program_id(0)
    shape = out_ref.shape[1:]
    r = lax.broadcasted_iota(jnp.int32, shape, 0)
    s = lax.broadcasted_iota(jnp.int32, shape, 1)
    q, k = (s, r) if transposed else (r, s)
    dist = off + q - k
    acc = jnp.zeros(shape, F32) + tab_ref[NUM_BUCKETS - 1, c]
    for b in range(NUM_BUCKETS - 2, -1, -1):
        if BUCKET_HI[b] is not None:
            acc = jnp.where(dist <= BUCKET_HI[b], tab_ref[b, c], acc)
    acc = (acc - tab_ref[NUM_BUCKETS - 1, c]) * LOG2E
    if causal:
        acc = jnp.where(k <= q, acc, NEG_INF)
    out_ref[0] = acc


def _toeplitz_bias(tab_pad, rows, cols, off, causal, transposed=False):
    n = tab_pad.shape[1]
    return pl.pallas_call(
        functools.partial(_toeplitz_kernel, off=off, causal=causal, transposed=transposed),
        out_shape=jax.ShapeDtypeStruct((n, rows, cols), F32),
        grid=(n,),
        in_specs=[pl.BlockSpec(memory_space=pltpu.SMEM)],
        out_specs=pl.BlockSpec((1, rows, cols), lambda c: (c, 0, 0)),
        compiler_params=_params(1),
        name="toeplitz_bias",
    )(tab_pad)


_E_GM = 0
_E_DQ = _E_GM + MLA_WIDTH
_E_DK = _E_DQ + DIFF_WIDTH
_E_DV = _E_DK + DIFF_KV_WIDTH
_E_GD = _E_DV + DIFF_KV_WIDTH
_E_CKV = _E_GD + DIFF_WIDTH
_E_CQ = _E_CKV + MLA_KV_LORA
_E_KA = _E_CQ + 2 * LANES
_E_KB = _E_KA + LANES
_E_END = _E_KB + LANES


def _even_in_kernel(x_ref, ng_ref, win_ref, qg_ref, wuq_ref, wukt_ref, kvg_ref, cos_ref, sin_ref,
                    qm_ref, kcat_ref, ckvt_ref, ckv_ref, kr_ref, gm_ref, dq_ref, dk_ref, dkb_ref, dv_ref,
                    dvt_ref, gd_ref):
    h = _rms(x_ref[...], ng_ref[...], NORM_EPS).astype(BF16)
    z = _dot(h, win_ref[...])
    cos2 = cos_ref[...]
    sin2 = sin_ref[...]
    gm_ref[...] = z[:, _E_GM:_E_DQ]
    gd_ref[...] = z[:, _E_GD:_E_CKV]
    dk = z[:, _E_DK:_E_DV]
    dv = z[:, _E_DV:_E_GD]
    dk_ref[...] = dk
    dkb_ref[...] = dk.astype(BF16)
    dv_ref[...] = dv
    dvt_ref[0] = dv.T.astype(BF16)
    lane = lax.broadcasted_iota(jnp.int32, (x_ref.shape[0], LANES), 1)
    dscale = DIFF_HEAD_DIM ** -0.5 * LOG2E
    group = DIFF_HEADS // DIFF_KV_HEADS
    for kv in range(DIFF_KV_HEADS):
        for g in range(group):
            c0 = _E_DQ + (kv * group + g) * 2 * DIFF_HEAD_DIM
            src = z[:, c0:c0 + 2 * DIFF_HEAD_DIM] * dscale
            dq_ref[kv, g] = jnp.where(lane < DIFF_HEAD_DIM, src, 0.0).astype(BF16)
            dq_ref[kv, group + g] = jnp.where(lane >= DIFF_HEAD_DIM, src, 0.0).astype(BF16)
    ckv = _rms(z[:, _E_CKV:_E_CQ], kvg_ref[...], NORM_EPS)
    kr = z[:, _E_KA:_E_KB] * cos2 + z[:, _E_KB:_E_END] * sin2
    ckv_ref[...] = ckv
    kr_ref[...] = kr[:, :MLA_ROPE]
    kcat_ref[...] = jnp.concatenate([ckv, kr], axis=-1).astype(BF16)
    ckvt_ref[0] = ckv.T.astype(BF16)
    cq = _rms(z[:, _E_CQ:_E_CQ + MLA_Q_LORA], qg_ref[...], NORM_EPS).astype(BF16)
    qall = _dot(cq, wuq_ref[...])
    qscale = (MLA_NOPE + MLA_ROPE) ** -0.5 * LOG2E
    nh = MLA_HEADS * LANES
    for hd in range(MLA_HEADS):
        qn = qall[:, hd * LANES:(hd + 1) * LANES].astype(BF16)
        ql = _dot(qn, wukt_ref[hd])
        qr = (qall[:, nh + hd * LANES:nh + (hd + 1) * LANES] * cos2
              + qall[:, 2 * nh + hd * LANES:2 * nh + (hd + 1) * LANES] * sin2)
        qm_ref[hd] = (jnp.concatenate([ql, qr], axis=-1) * qscale).astype(BF16)


def _even_in(x, ng, win, qg, wuq, wukt, kvg, cos2, sin2):
    n, d = x.shape
    tm = min(ROW_TILE, n)
    row = lambda w: pl.BlockSpec((tm, w), lambda i: (i, 0))
    tmajor = lambda w: pl.BlockSpec((1, w, tm), lambda i: (i, 0, 0))
    out_shape = (
        jax.ShapeDtypeStruct((MLA_HEADS, n, 2 * LANES), BF16),
        jax.ShapeDtypeStruct((n, 2 * LANES), BF16),
        jax.ShapeDtypeStruct((n // tm, MLA_KV_LORA, tm), BF16),
        jax.ShapeDtypeStruct((n, MLA_KV_LORA), F32),
        jax.ShapeDtypeStruct((n, MLA_ROPE), F32),
        jax.ShapeDtypeStruct((n, MLA_WIDTH), F32),
        jax.ShapeDtypeStruct((DIFF_KV_HEADS, 4, n, LANES), BF16),
        jax.ShapeDtypeStruct((n, DIFF_KV_WIDTH), F32),
        jax.ShapeDtypeStruct((n, DIFF_KV_WIDTH), BF16),
        jax.ShapeDtypeStruct((n, DIFF_KV_WIDTH), F32),
        jax.ShapeDtypeStruct((n // tm, DIFF_KV_WIDTH, tm), BF16),
        jax.ShapeDtypeStruct((n, DIFF_WIDTH), F32),
    )
    out_specs = (
        pl.BlockSpec((MLA_HEADS, tm, 2 * LANES), lambda i: (0, i, 0)),
        row(2 * LANES), tmajor(MLA_KV_LORA), row(MLA_KV_LORA), row(MLA_ROPE), row(MLA_WIDTH),
        pl.BlockSpec((DIFF_KV_HEADS, 4, tm, LANES), lambda i: (0, 0, i, 0)),
        row(DIFF_KV_WIDTH), row(DIFF_KV_WIDTH), row(DIFF_KV_WIDTH), tmajor(DIFF_KV_WIDTH), row(DIFF_WIDTH),
    )
    return pl.pallas_call(
        _even_in_kernel,
        out_shape=out_shape,
        grid=(n // tm,),
        in_specs=[row(d), _full(ng.shape), _full(win.shape), _full(qg.shape), _full(wuq.shape),
                  _full(wukt.shape), _full(kvg.shape), row(LANES), row(LANES)],
        out_specs=out_specs,
        compiler_params=_params(1),
        name="even_in",
    )(x, ng, win, qg, wuq, wukt, kvg, cos2, sin2)


def _flash_init(m_ref, l_ref, acc_ref):
    m_ref[...] = jnp.full(m_ref.shape, -jnp.inf, F32)
    l_ref[...] = jnp.zeros(l_ref.shape, F32)
    acc_ref[...] = jnp.zeros(acc_ref.shape, F32)


def _flash_update(s, v, m_ref, l_ref, acc_ref):
    m_prev = m_ref[...]
    m_new = jnp.maximum(m_prev, jnp.max(s, axis=-1, keepdims=True))
    alpha = jnp.exp2(m_prev - m_new)
    p = jnp.exp2(s - m_new)
    l_ref[...] = alpha * l_ref[...] + jnp.sum(p, axis=-1, keepdims=True)
    acc_ref[...] = alpha * acc_ref[...] + _dot(p.astype(BF16), v)
    m_ref[...] = m_new


def _flash_update_t(st, vt, m_ref, l_ref, acc_ref):
    m_prev = m_ref[...]
    m_new = jnp.maximum(m_prev, jnp.max(st, axis=0, keepdims=True))
    alpha = jnp.exp2(m_prev - m_new)
    p = jnp.exp2(st - m_new)
    l_ref[...] = alpha * l_ref[...] + jnp.sum(p, axis=0, keepdims=True)
    acc_ref[...] = alpha * acc_ref[...] + _dot(vt, p.astype(BF16))
    m_ref[...] = m_new


def _vt_tiles(vt_ref, j, w, rows):
    tiles = [vt_ref[j + u, rows, :] for u in range(w)]
    return tiles[0] if w == 1 else jnp.concatenate(tiles, axis=1)


def _far_tiles(n_far, tile):
    def body(jj, carry):
        tile(2 * jj, 2)
        return carry

    lax.fori_loop(0, n_far // 2, body, 0)

    @pl.when(n_far % 2 == 1)
    def _():
        tile(n_far - 1, 1)


def _mla_prompt_kernel(q_ref, k_ref, vt_ref, mask_ref, o_ref, m_ref, l_ref, acc_ref):
    i = pl.program_id(0)
    t = ATT_TILE
    q = q_ref[...].reshape(MLA_HEADS * t, 2 * LANES)
    _flash_init(m_ref, l_ref, acc_ref)

    def tile(j, w, mask=None):
        k = k_ref[pl.ds(pl.multiple_of(j * t, t), w * t), :]
        st = _dot_nt(k, q)
        if mask is not None:
            st = st + mask
        _flash_update_t(st, _vt_tiles(vt_ref, j, w, slice(None)), m_ref, l_ref, acc_ref)

    _far_tiles(i, tile)
    tile(i, 1, mask_ref[...])
    o = (acc_ref[...] / l_ref[...]).T
    o_ref[...] = o.reshape(MLA_HEADS, t, MLA_KV_LORA)


def _mla_prompt(qm, kcat, ckvt, mask):
    s = kcat.shape[0]
    t = ATT_TILE
    rows = MLA_HEADS * t
    return pl.pallas_call(
        _mla_prompt_kernel,
        out_shape=jax.ShapeDtypeStruct((MLA_HEADS, s, MLA_KV_LORA), F32),
        grid=(s // t,),
        in_specs=[pl.BlockSpec((MLA_HEADS, t, 2 * LANES), lambda i: (0, i, 0)),
                  _resident(kcat.shape), _resident(ckvt.shape), _resident(mask.shape)],
        out_specs=pl.BlockSpec((MLA_HEADS, t, MLA_KV_LORA), lambda i: (0, i, 0)),
        scratch_shapes=[pltpu.VMEM((1, rows), F32), pltpu.VMEM((1, rows), F32),
                        pltpu.VMEM((MLA_KV_LORA, rows), F32)],
        compiler_params=_params(1),
        name="mla_prompt",
    )(qm, kcat, ckvt, mask)


def _diff_lambda(lp, lambda_init):
    a = jnp.sum(lp[0:1] * lp[1:2], axis=-1, keepdims=True)
    b = jnp.sum(lp[2:3] * lp[3:4], axis=-1, keepdims=True)
    return jnp.exp(a) - jnp.exp(b) + lambda_init


def _diff_prompt_kernel(q_ref, k_ref, vt_ref, bias_ref, lp_ref, o_ref, m_ref, l_ref, acc_ref, *, lambda_init):
    i = pl.program_id(0)
    t = ATT_TILE
    group = DIFF_HEADS // DIFF_KV_HEADS
    rows = 2 * group * t
    lam = _diff_lambda(lp_ref[...], lambda_init)
    for kv in range(DIFF_KV_HEADS):
        q = q_ref[kv].reshape(rows, LANES)
        cols = slice(kv * LANES, (kv + 1) * LANES)
        _flash_init(m_ref, l_ref, acc_ref)

        def tile(j, w, bias=None, q=q, cols=cols):
            k = k_ref[pl.ds(pl.multiple_of(j * t, t), w * t), cols]
            st = _dot_nt(k, q)
            if bias is not None:
                st = st + bias
            _flash_update_t(st, _vt_tiles(vt_ref, j, w, cols), m_ref, l_ref, acc_ref)

        _far_tiles(jnp.maximum(i - 1, 0), tile)

        @pl.when(i >= 1)
        def _(tile=tile, kv=kv):
            tile(i - 1, 1, bias_ref[kv, 1])

        tile(i, 1, bias_ref[kv, 0])
        ot = acc_ref[...] / l_ref[...]
        o = (ot[:, :group * t] - lam * ot[:, group * t:]).T
        for g in range(group):
            hd = kv * group + g
            o_ref[:, hd * LANES:(hd + 1) * LANES] = o[g * t:(g + 1) * t]


def _diff_prompt(dq, dkb, dvt, bias, lp, lambda_init):
    s = dkb.shape[0]
    t = ATT_TILE
    rows = 4 * t
    return pl.pallas_call(
        functools.partial(_diff_prompt_kernel, lambda_init=lambda_init),
        out_shape=jax.ShapeDtypeStruct((s, DIFF_WIDTH), F32),
        grid=(s // t,),
        in_specs=[pl.BlockSpec((DIFF_KV_HEADS, 4, t, LANES), lambda i: (0, 0, i, 0)),
                  _resident(dkb.shape), _resident(dvt.shape), _resident(bias.shape), _full(lp.shape)],
        out_specs=pl.BlockSpec((t, DIFF_WIDTH), lambda i: (i, 0)),
        scratch_shapes=[pltpu.VMEM((1, rows), F32), pltpu.VMEM((1, rows), F32),
                        pltpu.VMEM((LANES, rows), F32)],
        compiler_params=_params(1),
        name="diff_prompt",
    )(dq, dkb, dvt, bias, lp)


_O_Q = 0
_O_K = _O_Q + MOBA_WIDTH
_O_V = _O_K + MOBA_KV_WIDTH
_O_G = _O_V + MOBA_KV_WIDTH
_O_END = _O_G + MOBA_WIDTH


def _mid_kernel(x_ref, olat_ref, wuv_ref, diffo_ref, gm_ref, gd_ref, subg_ref, wout_ref, ng_ref, win_ref,
                x1_ref, q_ref, k_ref, kb_ref, v_ref, vt_ref, g_ref, mean_ref, *, sub_scale):
    mla = jnp.concatenate([_dot(olat_ref[hd].astype(BF16), wuv_ref[hd]) for hd in range(MLA_HEADS)], axis=-1)
    dn = jnp.concatenate(
        [_rms(diffo_ref[:, hd * LANES:(hd + 1) * LANES], subg_ref[...], SUBLN_EPS) * sub_scale
         for hd in range(DIFF_HEADS)], axis=-1)
    mix = jnp.concatenate([mla * _silu(gm_ref[...]), dn * _silu(gd_ref[...])], axis=-1).astype(BF16)
    x1 = x_ref[...] + _dot(mix, wout_ref[...])
    x1_ref[...] = x1
    h = _rms(x1, ng_ref[...], NORM_EPS).astype(BF16)
    z = _dot(h, win_ref[...])
    q_ref[...] = z[:, _O_Q:_O_K]
    k = z[:, _O_K:_O_V]
    v = z[:, _O_V:_O_G]
    k_ref[...] = k
    kb_ref[...] = k.astype(BF16)
    v_ref[...] = v
    vt_ref[0] = v.T.astype(BF16)
    g_ref[...] = z[:, _O_G:_O_END]
    mean_ref[0] = jnp.mean(k, axis=0, keepdims=True)


def _mid(x, olat, wuv, diffo, gm, gd, subg, wout, ng, win, sub_scale):
    n, d = x.shape
    tm = min(ROW_TILE, n)
    row = lambda w: pl.BlockSpec((tm, w), lambda i: (i, 0))
    out_shape = (
        jax.ShapeDtypeStruct((n, d), F32), jax.ShapeDtypeStruct((n, MOBA_WIDTH), F32),
        jax.ShapeDtypeStruct((n, MOBA_KV_WIDTH), F32), jax.ShapeDtypeStruct((n, MOBA_KV_WIDTH), BF16),
        jax.ShapeDtypeStruct((n, MOBA_KV_WIDTH), F32), jax.ShapeDtypeStruct((n // tm, MOBA_KV_WIDTH, tm), BF16),
        jax.ShapeDtypeStruct((n, MOBA_WIDTH), F32), jax.ShapeDtypeStruct((n // tm, 1, MOBA_KV_WIDTH), F32),
    )
    out_specs = (row(d), row(MOBA_WIDTH), row(MOBA_KV_WIDTH), row(MOBA_KV_WIDTH), row(MOBA_KV_WIDTH),
                 pl.BlockSpec((1, MOBA_KV_WIDTH, tm), lambda i: (i, 0, 0)), row(MOBA_WIDTH),
                 pl.BlockSpec((1, 1, MOBA_KV_WIDTH), lambda i: (i, 0, 0)))
    return pl.pallas_call(
        functools.partial(_mid_kernel, sub_scale=sub_scale),
        out_shape=out_shape,
        grid=(n // tm,),
        in_specs=[row(d), pl.BlockSpec((MLA_HEADS, tm, MLA_KV_LORA), lambda i: (0, i, 0)), _full(wuv.shape),
                  row(DIFF_WIDTH), row(MLA_WIDTH), row(DIFF_WIDTH), _full(subg.shape), _full(wout.shape),
                  _full(ng.shape), _full(win.shape)],
        out_specs=out_specs,
        compiler_params=_params(1),
        name="even_out_odd_in",
    )(x, olat, wuv, diffo, gm, gd, subg, wout, ng, win)


def _topk_mask(gate, ksel):
    lane = lax.broadcasted_iota(jnp.int32, gate.shape, 1).astype(F32)
    sel = jnp.zeros(gate.shape, jnp.bool_)
    for _ in range(ksel):
        mx = jnp.max(gate, axis=-1, keepdims=True)
        idx = jnp.min(jnp.where(gate == mx, lane, float(gate.shape[1])), axis=-1, keepdims=True)
        pick = lane == idx
        sel = jnp.logical_or(sel, pick)
        gate = jnp.where(pick, -jnp.inf, gate)
    return sel


def _moba_prompt_kernel(q_ref, mean_ref, k_ref, vt_ref, bias_ref, o_ref, m_ref, l_ref, acc_ref, *, nblocks):
    i = pl.program_id(0)
    t = ATT_TILE
    nbp = mean_ref.shape[0]
    group = MOBA_HEADS // MOBA_KV_HEADS
    rows = group * t
    scale = MOBA_HEAD_DIM ** -0.5 * LOG2E
    blk = lax.broadcasted_iota(jnp.int32, (rows, nbp), 1)
    for kv in range(MOBA_KV_HEADS):
        cols = slice(kv * LANES, (kv + 1) * LANES)
        qf = jnp.concatenate(
            [q_ref[:, (kv * group + g) * LANES:(kv * group + g + 1) * LANES] for g in range(group)], axis=0)
        gate = lax.dot_general(qf, mean_ref[:, cols], (((1,), (1,)), ((), ())),
                               precision=lax.Precision.HIGHEST, preferred_element_type=F32)
        past = blk < i
        sel = jnp.logical_and(_topk_mask(jnp.where(past, gate, NEG_INF), min(MOBA_TOPK, nblocks)), past)
        q = jnp.concatenate([qf * scale, jnp.where(sel, 0.0, NEG_INF)], axis=-1).astype(BF16)
        _flash_init(m_ref, l_ref, acc_ref)

        def tile(j, w, bias=None, masked=True, q=q, cols=cols):
            k = k_ref[pl.ds(pl.multiple_of(j * t, t), w * t), cols]
            kblk = j + lax.broadcasted_iota(jnp.int32, (w * t, nbp), 0) // t
            hot = jnp.logical_and(lax.broadcasted_iota(jnp.int32, (w * t, nbp), 1) == kblk, masked)
            st = _dot_nt(jnp.concatenate([k, jnp.where(hot, 1.0, 0.0).astype(BF16)], axis=-1), q)
            if bias is not None:
                st = st + bias
            _flash_update_t(st, _vt_tiles(vt_ref, j, w, cols), m_ref, l_ref, acc_ref)

        _far_tiles(jnp.maximum(i - 1, 0), tile)

        @pl.when(i >= 1)
        def _(tile=tile, kv=kv):
            tile(i - 1, 1, bias_ref[kv, 1])

        tile(i, 1, bias_ref[kv, 0], masked=False)
        o = (acc_ref[...] / l_ref[...]).T
        for g in range(group):
            hd = kv * group + g
            o_ref[:, hd * LANES:(hd + 1) * LANES] = o[g * t:(g + 1) * t]


def _moba_prompt(q, means, kb, vt, bias, nblocks):
    s = kb.shape[0]
    t = ATT_TILE
    rows = (MOBA_HEADS // MOBA_KV_HEADS) * t
    return pl.pallas_call(
        functools.partial(_moba_prompt_kernel, nblocks=nblocks),
        out_shape=jax.ShapeDtypeStruct((s, MOBA_WIDTH), F32),
        grid=(s // t,),
        in_specs=[pl.BlockSpec((t, MOBA_WIDTH), lambda i: (i, 0)), _resident(means.shape),
                  _resident(kb.shape), _resident(vt.shape), _resident(bias.shape)],
        out_specs=pl.BlockSpec((t, MOBA_WIDTH), lambda i: (i, 0)),
        scratch_shapes=[pltpu.VMEM((1, rows), F32), pltpu.VMEM((1, rows), F32),
                        pltpu.VMEM((LANES, rows), F32)],
        compiler_params=_params(1),
        name="moba_prompt",
    )(q, means, kb, vt, bias)


def _final_kernel(x_ref, o_ref, g_ref, wout_ref, fg_ref, y_ref):
    mix = (o_ref[...] * _silu(g_ref[...])).astype(BF16)
    x2 = x_ref[...] + _dot(mix, wout_ref[...])
    y_ref[...] = _rms(x2, fg_ref[...], NORM_EPS)


def _final(x1, o, g, wout, fg):
    n, d = x1.shape
    tm = min(ROW_TILE, n)
    row = lambda w: pl.BlockSpec((tm, w), lambda i: (i, 0))
    return pl.pallas_call(
        _final_kernel,
        out_shape=jax.ShapeDtypeStruct((n, d), F32),
        grid=(n // tm,),
        in_specs=[row(d), row(MOBA_WIDTH), row(MOBA_WIDTH), _full(wout.shape), _full(fg.shape)],
        out_specs=row(d),
        compiler_params=_params(1),
        name="odd_out_final",
    )(x1, o, g, wout, fg)


def _page_specs(block, n, lead):
    def spec(p):
        return pl.BlockSpec((None,) * len(lead(0, 0)) + block,
                            lambda b, c, pt: lead(pt[b, c * n + p], 0) + (0,) * len(block))
    return [spec(p) for p in range(n)]


def _seq_spec(shape):
    return pl.BlockSpec((1,) + shape, lambda b, c, pt: (b,) + (0,) * len(shape))


def _const_spec(a):
    return pl.BlockSpec(a.shape, lambda b, c, pt: (0,) * a.ndim)


def _mla_sample_kernel(pt_ref, q_ref, knew_ref, mask_ref, *refs, n):
    ckv_refs, krt_refs = refs[:n], refs[n:2 * n]
    o_ref, m_ref, l_ref, acc_ref = refs[2 * n:]
    c = pl.program_id(1)
    q = q_ref[0]

    @pl.when(c == 0)
    def _():
        _flash_init(m_ref, l_ref, acc_ref)

    ckv = jnp.concatenate([r[...].astype(BF16) for r in ckv_refs], axis=0)
    krt = jnp.concatenate([r[...].astype(BF16) for r in krt_refs], axis=1)
    s = _dot_nt(q[:, :MLA_KV_LORA], ckv) + _dot(q[:, MLA_KV_LORA:MLA_KV_LORA + MLA_ROPE], krt)
    _flash_update(s, ckv, m_ref, l_ref, acc_ref)

    @pl.when(c == pl.num_programs(1) - 1)
    def _():
        knew = knew_ref[0]
        _flash_update(_dot_nt(q, knew) + mask_ref[...], knew[:, :MLA_KV_LORA], m_ref, l_ref, acc_ref)
        o_ref[0] = acc_ref[...] / l_ref[...]


def _mla_sample(page_table, q, knew, mask, cache_ckv, cache_krt, layer):
    nseq, rows, _ = q.shape
    n = min(PAGES_PER_STEP, page_table.shape[1])
    steps = page_table.shape[1] // n
    lead = lambda page, _: (layer, page)
    grid_spec = pltpu.PrefetchScalarGridSpec(
        num_scalar_prefetch=1,
        grid=(nseq, steps),
        in_specs=[_seq_spec((rows, 2 * LANES)), _seq_spec((LANES, 2 * LANES)), _const_spec(mask)]
        + _page_specs((PAGE_SIZE, MLA_KV_LORA), n, lead) + _page_specs((MLA_ROPE, PAGE_SIZE), n, lead),
        out_specs=_seq_spec((rows, MLA_KV_LORA)),
        scratch_shapes=[pltpu.VMEM((rows, 1), F32), pltpu.VMEM((rows, 1), F32),
                        pltpu.VMEM((rows, MLA_KV_LORA), F32)],
    )
    return pl.pallas_call(
        functools.partial(_mla_sample_kernel, n=n),
        out_shape=jax.ShapeDtypeStruct((nseq, rows, MLA_KV_LORA), F32),
        grid_spec=grid_spec,
        compiler_params=_params(2),
        name="mla_sample",
    )(page_table, q, knew, mask, *([cache_ckv] * n), *([cache_krt] * n))


def _diff_sample_kernel(pt_ref, q_ref, knew_ref, vnew_ref, pmask_ref, blast_ref, bnew_ref, lp_ref, *refs,
                        n, lambda_init):
    k_refs, v_refs = refs[:n], refs[n:2 * n]
    o_ref, m_ref, l_ref, acc_ref = refs[2 * n:]
    c = pl.program_id(1)
    last = pl.num_programs(1) - 1
    q = q_ref[0]

    @pl.when(c == 0)
    def _():
        _flash_init(m_ref, l_ref, acc_ref)

    def pages(bias_ref):
        k = jnp.concatenate([r[...].astype(BF16) for r in k_refs], axis=0)
        v = jnp.concatenate([r[...].astype(BF16) for r in v_refs], axis=0)
        _flash_update(_dot_nt(q, k) + bias_ref[...], v, m_ref, l_ref, acc_ref)

    @pl.when(c != last)
    def _():
        pages(pmask_ref)

    @pl.when(c == last)
    def _():
        pages(blast_ref)
        _flash_update(_dot_nt(q, knew_ref[0].astype(BF16)) + bnew_ref[...], vnew_ref[0].astype(BF16),
                      m_ref, l_ref, acc_ref)
        lam = _diff_lambda(lp_ref[...], lambda_init)
        o = acc_ref[...] / l_ref[...]
        per_kv = q.shape[0] // DIFF_KV_HEADS
        half = per_kv // 2
        for kv in range(DIFF_KV_HEADS):
            r0 = kv * per_kv
            o_ref[0, kv] = o[r0:r0 + half] - lam * o[r0 + half:r0 + per_kv]


def _diff_sample(page_table, q, knew, vnew, pmask, blast, bnew, lp, cache_k, cache_v, lambda_init):
    nseq, rows, _ = q.shape
    n = min(PAGES_PER_STEP, page_table.shape[1])
    steps = page_table.shape[1] // n
    prow = cache_k.shape[1]
    lead = lambda page, _: (page,)
    grid_spec = pltpu.PrefetchScalarGridSpec(
        num_scalar_prefetch=1,
        grid=(nseq, steps),
        in_specs=[_seq_spec((rows, LANES)), _seq_spec((prow, LANES)), _seq_spec((prow, LANES)),
                  _const_spec(pmask), _const_spec(blast), _const_spec(bnew), _const_spec(lp)]
        + _page_specs((prow, LANES), n, lead) + _page_specs((prow, LANES), n, lead),
        out_specs=_seq_spec((DIFF_KV_HEADS, rows // 4, LANES)),
        scratch_shapes=[pltpu.VMEM((rows, 1), F32), pltpu.VMEM((rows, 1), F32), pltpu.VMEM((rows, LANES), F32)],
    )
    return pl.pallas_call(
        functools.partial(_diff_sample_kernel, n=n, lambda_init=lambda_init),
        out_shape=jax.ShapeDtypeStruct((nseq, DIFF_KV_HEADS, rows // 4, LANES), F32),
        grid_spec=grid_spec,
        compiler_params=_params(2),
        name="diff_sample",
    )(page_table, q, knew, vnew, pmask, blast, bnew, lp, *([cache_k] * n), *([cache_v] * n))


def _moba_sample_kernel(pt_ref, q_ref, knew_ref, vnew_ref, pmask_ref, blast_ref, bnew_ref, *refs, n, nblocks):
    k_refs, v_refs = refs[:n], refs[n:2 * n]
    o_ref, gate_ref, mb_ref, lb_ref, ob_ref = refs[2 * n:]
    c = pl.program_id(1)
    last = pl.num_programs(1) - 1
    ppb = MOBA_BLOCK // PAGE_SIZE
    bps = n // ppb
    prow = k_refs[0].shape[0]
    brow = ppb * prow
    rows = q_ref.shape[1]
    qf = q_ref[0]
    qb = (qf * (MOBA_HEAD_DIM ** -0.5 * LOG2E)).astype(BF16)
    lane = lax.broadcasted_iota(jnp.int32, (rows, nblocks), 1)
    row_kv = lax.broadcasted_iota(jnp.int32, (rows, LANES), 0) // (rows // MOBA_KV_HEADS)
    key_kv = lax.broadcasted_iota(jnp.int32, (brow, LANES), 0) % MOBA_KV_HEADS

    @pl.when(c == 0)
    def _():
        for ref in (gate_ref, mb_ref, lb_ref):
            ref[...] = jnp.zeros(ref.shape, F32)

    def put(ref, blk, col):
        ref[...] = jnp.where(lane == blk, col, ref[...])

    def blocks(last_bias_ref):
        kf = [r[...] for r in k_refs]
        kb = jnp.concatenate([x.astype(BF16) for x in kf], axis=0)
        vb = jnp.concatenate([r[...].astype(BF16) for r in v_refs], axis=0)
        s_all = _dot_nt(qb, kb)
        for bi in range(bps):
            blk = c * bps + bi
            kblk = jnp.concatenate(kf[bi * ppb:(bi + 1) * ppb], axis=0)
            kmean = jnp.zeros((rows, LANES), F32)
            for kv in range(MOBA_KV_HEADS):
                mean_kv = jnp.sum(jnp.where(key_kv == kv, kblk, 0.0), axis=0, keepdims=True) * (1.0 / MOBA_BLOCK)
                kmean = jnp.where(row_kv == kv, mean_kv, kmean)
            put(gate_ref, blk, jnp.sum(qf * kmean, axis=-1, keepdims=True))
            bias_ref = last_bias_ref if bi == bps - 1 else pmask_ref
            s = s_all[:, bi * brow:(bi + 1) * brow] + bias_ref[...]
            m = jnp.max(s, axis=-1, keepdims=True)
            p = jnp.exp2(s - m)
            put(mb_ref, blk, m)
            put(lb_ref, blk, jnp.sum(p, axis=-1, keepdims=True))
            ob_ref[blk] = _dot(p.astype(BF16), vb[bi * brow:(bi + 1) * brow])

    @pl.when(c != last)
    def _():
        blocks(pmask_ref)

    @pl.when(c == last)
    def _():
        blocks(blast_ref)
        s = _dot_nt(qb, knew_ref[0].astype(BF16)) + bnew_ref[...]
        m_new = jnp.max(s, axis=-1, keepdims=True)
        p = jnp.exp2(s - m_new)
        l_new = jnp.sum(p, axis=-1, keepdims=True)
        o_new = _dot(p.astype(BF16), vnew_ref[0].astype(BF16))
        sel = _topk_mask(gate_ref[...], min(MOBA_TOPK, nblocks))
        mb = mb_ref[...]
        m_all = jnp.maximum(jnp.max(jnp.where(sel, mb, -jnp.inf), axis=-1, keepdims=True), m_new)
        w = jnp.where(sel, jnp.exp2(mb - m_all), 0.0)
        w_new = jnp.exp2(m_new - m_all)
        den = jnp.sum(w * lb_ref[...], axis=-1, keepdims=True) + w_new * l_new
        num = w_new * o_new
        for blk in range(nblocks):
            num = num + w[:, blk:blk + 1] * ob_ref[blk]
        o_ref[0] = num / den


def _moba_sample(page_table, q, knew, vnew, pmask, blast, bnew, cache_k, cache_v):
    nseq, rows, _ = q.shape
    ppb = MOBA_BLOCK // PAGE_SIZE
    nblocks = page_table.shape[1] // ppb
    n = min(PAGES_PER_STEP, page_table.shape[1])
    steps = page_table.shape[1] // n
    prow = cache_k.shape[1]
    lead = lambda page, _: (page,)
    grid_spec = pltpu.PrefetchScalarGridSpec(
        num_scalar_prefetch=1,
        grid=(nseq, steps),
        in_specs=[_seq_spec((rows, LANES)), _seq_spec((prow, LANES)), _seq_spec((prow, LANES)),
                  _const_spec(pmask), _const_spec(blast), _const_spec(bnew)]
        + _page_specs((prow, LANES), n, lead) + _page_specs((prow, LANES), n, lead),
        out_specs=_seq_spec((rows, LANES)),
        scratch_shapes=[pltpu.VMEM((rows, nblocks), F32), pltpu.VMEM((rows, nblocks), F32),
                        pltpu.VMEM((rows, nblocks), F32), pltpu.VMEM((nblocks, rows, LANES), F32)],
    )
    return pl.pallas_call(
        functools.partial(_moba_sample_kernel, n=n, nblocks=nblocks),
        out_shape=jax.ShapeDtypeStruct((nseq, rows, LANES), F32),
        grid_spec=grid_spec,
        compiler_params=_params(2),
        name="moba_sample",
    )(page_table, q, knew, vnew, pmask, blast, bnew, *([cache_k] * n), *([cache_v] * n))


def _rope_tables(pos):
    half = MLA_ROPE // 2
    inv = ROPE_BASE ** (-jnp.arange(half, dtype=F32) / half)
    ang = pos.astype(F32)[:, None] * inv
    pad = jnp.zeros((pos.shape[0], LANES - 2 * half), F32)
    cos = jnp.cos(ang)
    sin = jnp.sin(ang)
    return jnp.concatenate([cos, cos, pad], axis=-1), jnp.concatenate([sin, sin, pad], axis=-1)


def _lane_pad(w, width):
    return jnp.pad(w, ((0, 0), (0, width - w.shape[1])))


def _even_weights(w_in, w_uq, w_uk, w_uv):
    half = MLA_ROPE // 2
    sizes = [MLA_Q_LORA, MLA_KV_LORA, MLA_ROPE, MLA_WIDTH, DIFF_WIDTH, DIFF_KV_WIDTH, DIFF_KV_WIDTH]
    cq, ckv, kr, gm, dq, dk, dv, gd = jnp.split(w_in, np.cumsum(sizes), axis=1)
    ka = _lane_pad(kr, LANES)
    kb = _lane_pad(jnp.concatenate([-kr[:, half:], kr[:, :half]], axis=1), LANES)
    win = jnp.concatenate([gm, dq, dk, dv, gd, ckv, _lane_pad(cq, 2 * LANES), ka, kb], axis=1).astype(BF16)
    nope = jnp.concatenate([_lane_pad(w_uq[:, h, :MLA_NOPE], LANES) for h in range(MLA_HEADS)], axis=1)
    ra = jnp.concatenate([_lane_pad(w_uq[:, h, MLA_NOPE:], LANES) for h in range(MLA_HEADS)], axis=1)
    rb = jnp.concatenate(
        [_lane_pad(jnp.concatenate([-w_uq[:, h, MLA_NOPE + half:], w_uq[:, h, MLA_NOPE:MLA_NOPE + half]], axis=1),
                   LANES) for h in range(MLA_HEADS)], axis=1)
    wuq = jnp.concatenate([nope, ra, rb], axis=1).astype(BF16)
    wukt = jnp.pad(jnp.transpose(w_uk, (1, 2, 0)), ((0, 0), (0, LANES - MLA_NOPE), (0, 0))).astype(BF16)
    wuv = jnp.transpose(w_uv, (1, 0, 2)).astype(BF16)
    return win, wuq, wukt, wuv


def _interleave_kv(tile):
    nkv, r, j = tile.shape
    own = jnp.eye(nkv, dtype=jnp.bool_)[:, None, None, :]
    return jnp.where(own, tile[..., None], NEG_INF).reshape(nkv * r, j * nkv)


def kernel(x_prompt, x_sample, cache_mla_ckv, cache_mla_krope, cache_diff_k, cache_diff_v, cache_moba_k, cache_moba_v, page_table, norm_g, final_norm_g, rel_bias, w_in_even, mla_q_norm_g, mla_w_uq, mla_kv_norm_g, mla_w_uk, mla_w_uv, diff_lambda, diff_subln_g, w_out_even, w_in_odd, w_out_odd):
    _, s_len, d_model = x_prompt.shape
    n_dec, t_len, _ = x_sample.shape
    n_pages = page_table.shape[1]
    past_len = n_pages * PAGE_SIZE
    n_smp = n_dec * t_len
    t = ATT_TILE
    n_step = min(PAGES_PER_STEP, n_pages)
    assert norm_g.shape[0] == 2 and x_prompt.shape[0] == 1
    assert ROW_TILE == MOBA_BLOCK == ATT_TILE
    assert s_len % t == 0 and n_smp % min(ROW_TILE, n_smp) == 0 and past_len % MOBA_BLOCK == 0
    assert n_pages % n_step == 0 and n_step % (MOBA_BLOCK // PAGE_SIZE) == 0 and t_len <= 8
    assert FAR_DIST <= PAGE_SIZE + 1 and FAR_DIST <= t + 1
    dgroup = DIFF_HEADS // DIFF_KV_HEADS
    mgroup = MOBA_HEADS // MOBA_KV_HEADS
    nblocks_p = s_len // MOBA_BLOCK
    lambda_init = 0.8 - 0.6 * math.exp(-0.3 * 0)

    tab = jnp.pad(rel_bias.astype(F32), ((0, 0), (0, 8)))
    zero_col = rel_bias.shape[1]
    band0 = _toeplitz_bias(tab, t, t, 0, True, transposed=True)
    band1 = _toeplitz_bias(tab, t, t, t, False, transposed=True)
    s_last_page = _toeplitz_bias(tab, 8, PAGE_SIZE, PAGE_SIZE, False)
    s_last_block = _toeplitz_bias(tab, 8, MOBA_BLOCK, MOBA_BLOCK, False)
    s_new = _toeplitz_bias(tab, 8, LANES, 0, True)
    diff_cols = [[m * DIFF_HEADS + kv * dgroup + g for m in range(2) for g in range(dgroup)]
                 for kv in range(DIFF_KV_HEADS)]
    moba_cols = [[kv * mgroup + g for g in range(mgroup)] for kv in range(MOBA_KV_HEADS)]
    lanes_of = lambda tb, cols: jnp.stack([jnp.concatenate([tb[c] for c in cl], axis=1) for cl in cols])
    rows_of = lambda tb, cols: jnp.stack([jnp.concatenate([tb[c, :t_len] for c in cl], axis=0) for cl in cols])
    diff_band = jnp.stack([lanes_of(band0, diff_cols), lanes_of(band1, diff_cols)], axis=1)
    moba_band = jnp.stack([lanes_of(band0, moba_cols), lanes_of(band1, moba_cols)], axis=1)
    mla_mask = jnp.tile(band0[zero_col], (1, MLA_HEADS))
    mla_new_mask = jnp.tile(s_new[zero_col, :t_len], (MLA_HEADS, 1))
    drows = 2 * dgroup * t_len
    diff_pmask = _interleave_kv(jnp.zeros((DIFF_KV_HEADS, drows, n_step * PAGE_SIZE), F32))
    diff_last = jnp.concatenate([diff_pmask[:, :(n_step - 1) * 2 * PAGE_SIZE],
                                 _interleave_kv(rows_of(s_last_page, diff_cols))], axis=1)
    diff_new = _interleave_kv(rows_of(s_new, diff_cols))
    moba_pmask = _interleave_kv(jnp.zeros((MOBA_KV_HEADS, mgroup * t_len, MOBA_BLOCK), F32))
    moba_last = _interleave_kv(rows_of(s_last_block, moba_cols))
    moba_new = _interleave_kv(rows_of(s_new, moba_cols))

    win_e, wuq, wukt, wuv = _even_weights(w_in_even[0], mla_w_uq[0], mla_w_uk[0], mla_w_uv[0])
    wout_e = w_out_even[0].astype(BF16)
    win_o = w_in_odd[0].astype(BF16)
    wout_o = w_out_odd[0].astype(BF16)
    ng0, ng1, fg = norm_g[0][None], norm_g[1][None], final_norm_g[None]
    qg, kvg, subg = mla_q_norm_g[0][None], mla_kv_norm_g[0][None], diff_subln_g[0][None]
    lp = diff_lambda[0].astype(F32)

    xp = x_prompt[0]
    xs = x_sample.reshape(n_smp, d_model)
    cos_p, sin_p = _rope_tables(jnp.arange(s_len))
    cos_s, sin_s = _rope_tables(jnp.tile(past_len + jnp.arange(t_len), n_dec))

    krt_cache = jnp.swapaxes(cache_mla_krope, 2, 3)
    kv_rows = lambda a: a[0].reshape(a.shape[1], a.shape[2] * a.shape[3], a.shape[4])
    new_rows = lambda a, nkv: jnp.pad(a.reshape(n_dec, t_len * nkv, LANES),
                                      ((0, 0), (0, (PAGE_SIZE - t_len) * nkv), (0, 0)))

    (qm_p, kcat_p, ckvt_p, ckv_p, kr_p, gm_p, dq_p, dk_p, dkb_p, dv_p, dvt_p, gd_p) = _even_in(
        xp, ng0, win_e, qg, wuq, wukt, kvg, cos_p, sin_p)
    olat_p = _mla_prompt(qm_p, kcat_p, ckvt_p, mla_mask)
    diffo_p = _diff_prompt(dq_p, dkb_p, dvt_p, diff_band, lp, lambda_init)
    (x1_p, q_p, k_p, kb_p, v_p, vt_p, g_p, means_p) = _mid(
        xp, olat_p, wuv, diffo_p, gm_p, gd_p, subg, wout_e, ng1, win_o, 1.0 - lambda_init)

    (qm_s, kcat_s, _, ckv_s, kr_s, gm_s, dq_s, dk_s, _, dv_s, _, gd_s) = _even_in(
        xs, ng0, win_e, qg, wuq, wukt, kvg, cos_s, sin_s)
    q_mla = qm_s.reshape(MLA_HEADS, n_dec, t_len, 2 * LANES).transpose(1, 0, 2, 3).reshape(
        n_dec, MLA_HEADS * t_len, 2 * LANES)
    knew_mla = jnp.pad(kcat_s.reshape(n_dec, t_len, 2 * LANES), ((0, 0), (0, LANES - t_len), (0, 0)))
    olat_s = _mla_sample(page_table, q_mla, knew_mla, mla_new_mask, cache_mla_ckv, krt_cache, 0)
    olat_s = olat_s.reshape(n_dec, MLA_HEADS, t_len, MLA_KV_LORA).transpose(1, 0, 2, 3).reshape(
        MLA_HEADS, n_smp, MLA_KV_LORA)
    q_diff = dq_s.reshape(DIFF_KV_HEADS, 4, n_dec, t_len, LANES).transpose(2, 0, 1, 3, 4).reshape(
        n_dec, DIFF_KV_HEADS * drows, LANES)
    diffo_s = _diff_sample(page_table, q_diff, new_rows(dk_s, DIFF_KV_HEADS), new_rows(dv_s, DIFF_KV_HEADS),
                           diff_pmask, diff_last, diff_new, lp, kv_rows(cache_diff_k), kv_rows(cache_diff_v),
                           lambda_init)
    diffo_s = diffo_s.reshape(n_dec, DIFF_KV_HEADS, dgroup, t_len, LANES).transpose(0, 3, 1, 2, 4).reshape(
        n_smp, DIFF_WIDTH)
    (x1_s, q_s, k_s, _, v_s, _, g_s, _) = _mid(
        xs, olat_s, wuv, diffo_s, gm_s, gd_s, subg, wout_e, ng1, win_o, 1.0 - lambda_init)

    means = means_p.reshape(nblocks_p, MOBA_KV_WIDTH)
    means = jnp.pad(means, ((0, -nblocks_p % LANES), (0, 0)))
    o_p = _moba_prompt(q_p, means, kb_p, vt_p, moba_band, nblocks_p)
    y_p = _final(x1_p, o_p, g_p, wout_o, fg)

    q_moba = q_s.reshape(n_dec, t_len, MOBA_KV_HEADS, mgroup, LANES).transpose(0, 2, 3, 1, 4).reshape(
        n_dec, MOBA_HEADS * t_len, LANES)
    o_s = _moba_sample(page_table, q_moba, new_rows(k_s, MOBA_KV_HEADS), new_rows(v_s, MOBA_KV_HEADS),
                       moba_pmask, moba_last, moba_new, kv_rows(cache_moba_k), kv_rows(cache_moba_v))
    o_s = o_s.reshape(n_dec, MOBA_KV_HEADS, mgroup, t_len, LANES).transpose(0, 3, 1, 2, 4).reshape(
        n_smp, MOBA_WIDTH)
    y_s = _final(x1_s, o_s, g_s, wout_o, fg)

    kv4 = lambda a, b, s: a.reshape(1, b, s, 2, LANES)
    return (y_p[None], y_s.reshape(n_dec, t_len, d_model),
            ckv_p.reshape(1, 1, s_len, MLA_KV_LORA), kr_p.reshape(1, 1, s_len, MLA_ROPE),
            kv4(dk_p, 1, s_len), kv4(dv_p, 1, s_len), kv4(k_p, 1, s_len), kv4(v_p, 1, s_len),
            ckv_s.reshape(1, n_dec, t_len, MLA_KV_LORA), kr_s.reshape(1, n_dec, t_len, MLA_ROPE),
            kv4(dk_s, n_dec, t_len), kv4(dv_s, n_dec, t_len), kv4(k_s, n_dec, t_len), kv4(v_s, n_dec, t_len))
```

```python
import functools
import math

import jax
import jax.numpy as jnp
import numpy as np
from jax import lax
from jax.experimental import pallas as pl
from jax.experimental.pallas import tpu as pltpu

F32 = jnp.float32
BF16 = jnp.bfloat16

PAGE_SIZE = 128
MLA_HEADS = 8
MLA_Q_LORA = 192
MLA_KV_LORA = 128
MLA_NOPE = 64
MLA_ROPE = 32
MLA_V = 64
ROPE_BASE = 10000.0
DIFF_HEADS = 4
DIFF_KV_HEADS = 2
DIFF_HEAD_DIM = 64
SUBLN_EPS = 1e-5
MOBA_HEADS = 8
MOBA_KV_HEADS = 2
MOBA_HEAD_DIM = 128
MOBA_BLOCK = 256
MOBA_TOPK = 3
NUM_BUCKETS = 32
REL_MAX_DISTANCE = 128
REL_MAX_EXACT = NUM_BUCKETS // 2
NORM_EPS = 1e-6
NEG_INF = -1e30
LOG2E = math.log2(math.e)

LANES = 128
ROW_TILE = 256
ATT_TILE = 256
PAGES_PER_STEP = 16
VMEM_LIMIT = 56 * 1024 * 1024

MLA_WIDTH = MLA_HEADS * MLA_V
DIFF_WIDTH = DIFF_HEADS * 2 * DIFF_HEAD_DIM
DIFF_KV_WIDTH = DIFF_KV_HEADS * 2 * DIFF_HEAD_DIM
MOBA_WIDTH = MOBA_HEADS * MOBA_HEAD_DIM
MOBA_KV_WIDTH = MOBA_KV_HEADS * MOBA_HEAD_DIM


def _bucket_upper_bounds():
    d = np.arange(0, 4 * REL_MAX_DISTANCE)
    v = np.log(np.maximum(d, 1) / REL_MAX_EXACT) / math.log(REL_MAX_DISTANCE / REL_MAX_EXACT)
    v = v * (NUM_BUCKETS - REL_MAX_EXACT)
    frac = np.abs(v[REL_MAX_EXACT + 1:REL_MAX_DISTANCE] - np.round(v[REL_MAX_EXACT + 1:REL_MAX_DISTANCE]))
    assert frac.min() > 1e-3
    b = np.where(d < REL_MAX_EXACT, d, np.minimum(REL_MAX_EXACT + np.floor(v).astype(np.int64), NUM_BUCKETS - 1))
    hi = [int(d[b == k].max()) if np.any(b == k) else None for k in range(NUM_BUCKETS)]
    far = int(hi[NUM_BUCKETS - 2]) + 1
    return hi, far


BUCKET_HI, FAR_DIST = _bucket_upper_bounds()


def _dot(a, b):
    return jnp.dot(a, b, preferred_element_type=F32)


def _dot_nt(a, b):
    return lax.dot_general(a, b, (((1,), (1,)), ((), ())), preferred_element_type=F32)


def _rms(x, g, eps):
    return x * lax.rsqrt(jnp.mean(x * x, axis=-1, keepdims=True) + eps) * g


def _silu(g):
    return g / (1.0 + jnp.exp(-g))


def _full(shape):
    nd = len(shape)
    return pl.BlockSpec(shape, lambda *_: (0,) * nd)


def _resident(shape):
    nd = len(shape)
    return pl.BlockSpec(shape, lambda *_: (0,) * nd, pipeline_mode=pl.Buffered(1))


def _params(n_axes):
    return pltpu.CompilerParams(dimension_semantics=("arbitrary",) * n_axes, vmem_limit_bytes=VMEM_LIMIT)


def _toeplitz_kernel(tab_ref, out_ref, *, off, causal, transposed):
    c = pl.program_id(0)
    shape = out_ref.shape[1:]
    r = lax.broadcasted_iota(jnp.int32, shape, 0)
    s = lax.broadcasted_iota(jnp.int32, shape, 1)
    q, k = (s, r) if transposed else (r, s)
    dist = off + q - k
    acc = jnp.zeros(shape, F32) + tab_ref[NUM_BUCKETS - 1, c]
    for b in range(NUM_BUCKETS - 2, -1, -1):
        if BUCKET_HI[b] is not None:
            acc = jnp.where(dist <= BUCKET_HI[b], tab_ref[b, c], acc)
    acc = (acc - tab_ref[NUM_BUCKETS - 1, c]) * LOG2E
    if causal:
        acc = jnp.where(k <= q, acc, NEG_INF)
    out_ref[0] = acc


def _toeplitz_bias(tab_pad, rows, cols, off, causal, transposed=False):
    n = tab_pad.shape[1]
    return pl.pallas_call(
        functools.partial(_toeplitz_kernel, off=off, causal=causal, transposed=transposed),
        out_shape=jax.ShapeDtypeStruct((n, rows, cols), F32),
        grid=(n,),
        in_specs=[pl.BlockSpec(memory_space=pltpu.SMEM)],
        out_specs=pl.BlockSpec((1, rows, cols), lambda c: (c, 0, 0)),
        compiler_params=_params(1),
        name="toeplitz_bias",
    )(tab_pad)


_E_GM = 0
_E_DQ = _E_GM + MLA_WIDTH
_E_DK = _E_DQ + DIFF_WIDTH
_E_DV = _E_DK + DIFF_KV_WIDTH
_E_GD = _E_DV + DIFF_KV_WIDTH
_E_CKV = _E_GD + DIFF_WIDTH
_E_CQ = _E_CKV + MLA_KV_LORA
_E_KA = _E_CQ + 2 * LANES
_E_KB = _E_KA + LANES
_E_END = _E_KB + LANES


def _even_in_kernel(x_ref, ng_ref, win_ref, qg_ref, wuq_ref, wukt_ref, kvg_ref, cos_ref, sin_ref,
                    qm_ref, kcat_ref, ckvt_ref, ckv_ref, kr_ref, gm_ref, dq_ref, dk_ref, dkb_ref, dv_ref,
                    dvt_ref, gd_ref):
    h = _rms(x_ref[...], ng_ref[...], NORM_EPS).astype(BF16)
    z = _dot(h, win_ref[...])
    cos2 = cos_ref[...]
    sin2 = sin_ref[...]
    gm_ref[...] = z[:, _E_GM:_E_DQ]
    gd_ref[...] = z[:, _E_GD:_E_CKV]
    dk = z[:, _E_DK:_E_DV]
    dv = z[:, _E_DV:_E_GD]
    dk_ref[...] = dk
    dkb_ref[...] = dk.astype(BF16)
    dv_ref[...] = dv
    dvt_ref[0] = dv.T.astype(BF16)
    lane = lax.broadcasted_iota(jnp.int32, (x_ref.shape[0], LANES), 1)
    dscale = DIFF_HEAD_DIM ** -0.5 * LOG2E
    group = DIFF_HEADS // DIFF_KV_HEADS
    for kv in range(DIFF_KV_HEADS):
        for g in range(group):
            c0 = _E_DQ + (kv * group + g) * 2 * DIFF_HEAD_DIM
            src = z[:, c0:c0 + 2 * DIFF_HEAD_DIM] * dscale
            dq_ref[kv, g] = jnp.where(lane < DIFF_HEAD_DIM, src, 0.0).astype(BF16)
            dq_ref[kv, group + g] = jnp.where(lane >= DIFF_HEAD_DIM, src, 0.0).astype(BF16)
    ckv = _rms(z[:, _E_CKV:_E_CQ], kvg_ref[...], NORM_EPS)
    kr = z[:, _E_KA:_E_KB] * cos2 + z[:, _E_KB:_E_END] * sin2
    ckv_ref[...] = ckv
    kr_ref[...] = kr[:, :MLA_ROPE]
    kcat_ref[...] = jnp.concatenate([ckv, kr], axis=-1).astype(BF16)
    ckvt_ref[0] = ckv.T.astype(BF16)
    cq = _rms(z[:, _E_CQ:_E_CQ + MLA_Q_LORA], qg_ref[...], NORM_EPS).astype(BF16)
    qall = _dot(cq, wuq_ref[...])
    qscale = (MLA_NOPE + MLA_ROPE) ** -0.5 * LOG2E
    nh = MLA_HEADS * LANES
    for hd in range(MLA_HEADS):
        qn = qall[:, hd * LANES:(hd + 1) * LANES].astype(BF16)
        ql = _dot(qn, wukt_ref[hd])
        qr = (qall[:, nh + hd * LANES:nh + (hd + 1) * LANES] * cos2
              + qall[:, 2 * nh + hd * LANES:2 * nh + (hd + 1) * LANES] * sin2)
        qm_ref[hd] = (jnp.concatenate([ql, qr], axis=-1) * qscale).astype(BF16)


def _even_in(x, ng, win, qg, wuq, wukt, kvg, cos2, sin2):
    n, d = x.shape
    tm = min(ROW_TILE, n)
    row = lambda w: pl.BlockSpec((tm, w), lambda i: (i, 0))
    tmajor = lambda w: pl.BlockSpec((1, w, tm), lambda i: (i, 0, 0))
    out_shape = (
        jax.ShapeDtypeStruct((MLA_HEADS, n, 2 * LANES), BF16),
        jax.ShapeDtypeStruct((n, 2 * LANES), BF16),
        jax.ShapeDtypeStruct((n // tm, MLA_KV_LORA, tm), BF16),
        jax.ShapeDtypeStruct((n, MLA_KV_LORA), F32),
        jax.ShapeDtypeStruct((n, MLA_ROPE), F32),
        jax.ShapeDtypeStruct((n, MLA_WIDTH), F32),
        jax.ShapeDtypeStruct((DIFF_KV_HEADS, 4, n, LANES), BF16),
        jax.ShapeDtypeStruct((n, DIFF_KV_WIDTH), F32),
        jax.ShapeDtypeStruct((n, DIFF_KV_WIDTH), BF16),
        jax.ShapeDtypeStruct((n, DIFF_KV_WIDTH), F32),
        jax.ShapeDtypeStruct((n // tm, DIFF_KV_WIDTH, tm), BF16),
        jax.ShapeDtypeStruct((n, DIFF_WIDTH), F32),
    )
    out_specs = (
        pl.BlockSpec((MLA_HEADS, tm, 2 * LANES), lambda i: (0, i, 0)),
        row(2 * LANES), tmajor(MLA_KV_LORA), row(MLA_KV_LORA), row(MLA_ROPE), row(MLA_WIDTH),
        pl.BlockSpec((DIFF_KV_HEADS, 4, tm, LANES), lambda i: (0, 0, i, 0)),
        row(DIFF_KV_WIDTH), row(DIFF_KV_WIDTH), row(DIFF_KV_WIDTH), tmajor(DIFF_KV_WIDTH), row(DIFF_WIDTH),
    )
    return pl.pallas_call(
        _even_in_kernel,
        out_shape=out_shape,
        grid=(n // tm,),
        in_specs=[row(d), _full(ng.shape), _full(win.shape), _full(qg.shape), _full(wuq.shape),
                  _full(wukt.shape), _full(kvg.shape), row(LANES), row(LANES)],
        out_specs=out_specs,
        compiler_params=_params(1),
        name="even_in",
    )(x, ng, win, qg, wuq, wukt, kvg, cos2, sin2)


def _flash_init(m_ref, l_ref, acc_ref):
    m_ref[...] = jnp.full(m_ref.shape, -jnp.inf, F32)
    l_ref[...] = jnp.zeros(l_ref.shape, F32)
    acc_ref[...] = jnp.zeros(acc_ref.shape, F32)


def _flash_update(s, v, m_ref, l_ref, acc_ref):
    m_prev = m_ref[...]
    m_new = jnp.maximum(m_prev, jnp.max(s, axis=-1, keepdims=True))
    alpha = jnp.exp2(m_prev - m_new)
    p = jnp.exp2(s - m_new)
    l_ref[...] = alpha * l_ref[...] + jnp.sum(p, axis=-1, keepdims=True)
    acc_ref[...] = alpha * acc_ref[...] + _dot(p.astype(BF16), v)
    m_ref[...] = m_new


def _flash_update_t(st, vt, m_ref, l_ref, acc_ref):
    m_prev = m_ref[...]
    m_new = jnp.maximum(m_prev, jnp.max(st, axis=0, keepdims=True))
    alpha = jnp.exp2(m_prev - m_new)
    p = jnp.exp2(st - m_new)
    l_ref[...] = alpha * l_ref[...] + jnp.sum(p, axis=0, keepdims=True)
    acc_ref[...] = alpha * acc_ref[...] + _dot(vt, p.astype(BF16))
    m_ref[...] = m_new


def _vt_tiles(vt_ref, j, w, rows):
    tiles = [vt_ref[j + u, rows, :] for u in range(w)]
    return tiles[0] if w == 1 else jnp.concatenate(tiles, axis=1)


def _far_tiles(n_far, logits, update, st0_ref, st1_ref):
    npairs = n_far // 2
    last = jnp.maximum(npairs - 1, 0)

    @pl.when(npairs > 0)
    def _():
        st0_ref[...] = logits(0, 2)

    def body(u, carry):
        a = 2 * u
        st1_ref[...] = logits(2 * jnp.minimum(a + 1, last), 2)
        update(st0_ref[...], 2 * a, 2)

        @pl.when(a + 1 < npairs)
        def _():
            st0_ref[...] = logits(2 * jnp.minimum(a + 2, last), 2)
            update(st1_ref[...], 2 * (a + 1), 2)

        return carry

    lax.fori_loop(0, (npairs + 1) // 2, body, 0)

    @pl.when(n_far % 2 == 1)
    def _():
        update(logits(n_far - 1, 1), n_far - 1, 1)


def _mla_prompt_kernel(q_ref, k_ref, vt_ref, mask_ref, o_ref, m_ref, l_ref, acc_ref, st0_ref, st1_ref):
    i = pl.program_id(0)
    t = ATT_TILE
    q = q_ref[...].reshape(MLA_HEADS * t, 2 * LANES)
    _flash_init(m_ref, l_ref, acc_ref)

    def logits(j, w):
        return _dot_nt(k_ref[pl.ds(pl.multiple_of(j * t, t), w * t), :], q)

    def update(st, j, w):
        _flash_update_t(st, _vt_tiles(vt_ref, j, w, slice(None)), m_ref, l_ref, acc_ref)

    _far_tiles(i, logits, update, st0_ref, st1_ref)
    update(logits(i, 1) + mask_ref[...], i, 1)
    o = (acc_ref[...] / l_ref[...]).T
    o_ref[...] = o.reshape(MLA_HEADS, t, MLA_KV_LORA)


def _mla_prompt(qm, kcat, ckvt, mask):
    s = kcat.shape[0]
    t = ATT_TILE
    rows = MLA_HEADS * t
    return pl.pallas_call(
        _mla_prompt_kernel,
        out_shape=jax.ShapeDtypeStruct((MLA_HEADS, s, MLA_KV_LORA), F32),
        grid=(s // t,),
        in_specs=[pl.BlockSpec((MLA_HEADS, t, 2 * LANES), lambda i: (0, i, 0)),
                  _resident(kcat.shape), _resident(ckvt.shape), _resident(mask.shape)],
        out_specs=pl.BlockSpec((MLA_HEADS, t, MLA_KV_LORA), lambda i: (0, i, 0)),
        scratch_shapes=[pltpu.VMEM((1, rows), F32), pltpu.VMEM((1, rows), F32),
                        pltpu.VMEM((MLA_KV_LORA, rows), F32),
                        pltpu.VMEM((2 * t, rows), F32), pltpu.VMEM((2 * t, rows), F32)],
        compiler_params=_params(1),
        name="mla_prompt",
    )(qm, kcat, ckvt, mask)


def _diff_lambda(lp, lambda_init):
    a = jnp.sum(lp[0:1] * lp[1:2], axis=-1, keepdims=True)
    b = jnp.sum(lp[2:3] * lp[3:4], axis=-1, keepdims=True)
    return jnp.exp(a) - jnp.exp(b) + lambda_init


def _diff_prompt_kernel(q_ref, k_ref, vt_ref, bias_ref, lp_ref, o_ref, m_ref, l_ref, acc_ref, st0_ref, st1_ref, *,
                        lambda_init):
    i = pl.program_id(0)
    t = ATT_TILE
    group = DIFF_HEADS // DIFF_KV_HEADS
    rows = 2 * group * t
    lam = _diff_lambda(lp_ref[...], lambda_init)
    for kv in range(DIFF_KV_HEADS):
        q = q_ref[kv].reshape(rows, LANES)
        cols = slice(kv * LANES, (kv + 1) * LANES)
        _flash_init(m_ref, l_ref, acc_ref)

        def logits(j, w, q=q, cols=cols):
            return _dot_nt(k_ref[pl.ds(pl.multiple_of(j * t, t), w * t), cols], q)

        def update(st, j, w, cols=cols):
            _flash_update_t(st, _vt_tiles(vt_ref, j, w, cols), m_ref, l_ref, acc_ref)

        _far_tiles(jnp.maximum(i - 1, 0), logits, update, st0_ref, st1_ref)

        @pl.when(i >= 1)
        def _(logits=logits, update=update, kv=kv):
            update(logits(i - 1, 1) + bias_ref[kv, 1], i - 1, 1)

        update(logits(i, 1) + bias_ref[kv, 0], i, 1)
        ot = acc_ref[...] / l_ref[...]
        o = (ot[:, :group * t] - lam * ot[:, group * t:]).T
        for g in range(group):
            hd = kv * group + g
            o_ref[:, hd * LANES:(hd + 1) * LANES] = o[g * t:(g + 1) * t]


def _diff_prompt(dq, dkb, dvt, bias, lp, lambda_init):
    s = dkb.shape[0]
    t = ATT_TILE
    rows = 4 * t
    return pl.pallas_call(
        functools.partial(_diff_prompt_kernel, lambda_init=lambda_init),
        out_shape=jax.ShapeDtypeStruct((s, DIFF_WIDTH), F32),
        grid=(s // t,),
        in_specs=[pl.BlockSpec((DIFF_KV_HEADS, 4, t, LANES), lambda i: (0, 0, i, 0)),
                  _resident(dkb.shape), _resident(dvt.shape), _resident(bias.shape), _full(lp.shape)],
        out_specs=pl.BlockSpec((t, DIFF_WIDTH), lambda i: (i, 0)),
        scratch_shapes=[pltpu.VMEM((1, rows), F32), pltpu.VMEM((1, rows), F32),
                        pltpu.VMEM((LANES, rows), F32),
                        pltpu.VMEM((2 * t, rows), F32), pltpu.VMEM((2 * t, rows), F32)],
        compiler_params=_params(1),
        name="diff_prompt",
    )(dq, dkb, dvt, bias, lp)


_O_Q = 0
_O_K = _O_Q + MOBA_WIDTH
_O_V = _O_K + MOBA_KV_WIDTH
_O_G = _O_V + MOBA_KV_WIDTH
_O_END = _O_G + MOBA_WIDTH


def _mid_kernel(x_ref, olat_ref, wuv_ref, diffo_ref, gm_ref, gd_ref, subg_ref, wout_ref, ng_ref, win_ref,
                x1_ref, q_ref, k_ref, kb_ref, v_ref, vt_ref, g_ref, mean_ref, *, sub_scale):
    mla = jnp.concatenate([_dot(olat_ref[hd].astype(BF16), wuv_ref[hd]) for hd in range(MLA_HEADS)], axis=-1)
    dn = jnp.concatenate(
        [_rms(diffo_ref[:, hd * LANES:(hd + 1) * LANES], subg_ref[...], SUBLN_EPS) * sub_scale
         for hd in range(DIFF_HEADS)], axis=-1)
    mix = jnp.concatenate([mla * _silu(gm_ref[...]), dn * _silu(gd_ref[...])], axis=-1).astype(BF16)
    x1 = x_ref[...] + _dot(mix, wout_ref[...])
    x1_ref[...] = x1
    h = _rms(x1, ng_ref[...], NORM_EPS).astype(BF16)
    z = _dot(h, win_ref[...])
    q_ref[...] = z[:, _O_Q:_O_K]
    k = z[:, _O_K:_O_V]
    v = z[:, _O_V:_O_G]
    k_ref[...] = k
    kb_ref[...] = k.astype(BF16)
    v_ref[...] = v
    vt_ref[0] = v.T.astype(BF16)
    g_ref[...] = z[:, _O_G:_O_END]
    mean_ref[0] = jnp.mean(k, axis=0, keepdims=True)


def _mid(x, olat, wuv, diffo, gm, gd, subg, wout, ng, win, sub_scale):
    n, d = x.shape
    tm = min(ROW_TILE, n)
    row = lambda w: pl.BlockSpec((tm, w), lambda i: (i, 0))
    out_shape = (
        jax.ShapeDtypeStruct((n, d), F32), jax.ShapeDtypeStruct((n, MOBA_WIDTH), F32),
        jax.ShapeDtypeStruct((n, MOBA_KV_WIDTH), F32), jax.ShapeDtypeStruct((n, MOBA_KV_WIDTH), BF16),
        jax.ShapeDtypeStruct((n, MOBA_KV_WIDTH), F32), jax.ShapeDtypeStruct((n // tm, MOBA_KV_WIDTH, tm), BF16),
        jax.ShapeDtypeStruct((n, MOBA_WIDTH), F32), jax.ShapeDtypeStruct((n // tm, 1, MOBA_KV_WIDTH), F32),
    )
    out_specs = (row(d), row(MOBA_WIDTH), row(MOBA_KV_WIDTH), row(MOBA_KV_WIDTH), row(MOBA_KV_WIDTH),
                 pl.BlockSpec((1, MOBA_KV_WIDTH, tm), lambda i: (i, 0, 0)), row(MOBA_WIDTH),
                 pl.BlockSpec((1, 1, MOBA_KV_WIDTH), lambda i: (i, 0, 0)))
    return pl.pallas_call(
        functools.partial(_mid_kernel, sub_scale=sub_scale),
        out_shape=out_shape,
        grid=(n // tm,),
        in_specs=[row(d), pl.BlockSpec((MLA_HEADS, tm, MLA_KV_LORA), lambda i: (0, i, 0)), _full(wuv.shape),
                  row(DIFF_WIDTH), row(MLA_WIDTH), row(DIFF_WIDTH), _full(subg.shape), _full(wout.shape),
                  _full(ng.shape), _full(win.shape)],
        out_specs=out_specs,
        compiler_params=_params(1),
        name="even_out_odd_in",
    )(x, olat, wuv, diffo, gm, gd, subg, wout, ng, win)


def _topk_mask(gate, ksel):
    lane = lax.broadcasted_iota(jnp.int32, gate.shape, 1).astype(F32)
    sel = jnp.zeros(gate.shape, jnp.bool_)
    for _ in range(ksel):
        mx = jnp.max(gate, axis=-1, keepdims=True)
        idx = jnp.min(jnp.where(gate == mx, lane, float(gate.shape[1])), axis=-1, keepdims=True)
        pick = lane == idx
        sel = jnp.logical_or(sel, pick)
        gate = jnp.where(pick, -jnp.inf, gate)
    return sel


def _moba_prompt_kernel(q_ref, mean_ref, k_ref, vt_ref, bias_ref, o_ref, m_ref, l_ref, acc_ref, st0_ref, st1_ref, *,
                        nblocks):
    i = pl.program_id(0)
    t = ATT_TILE
    nbp = mean_ref.shape[0]
    group = MOBA_HEADS // MOBA_KV_HEADS
    rows = group * t
    scale = MOBA_HEAD_DIM ** -0.5 * LOG2E
    blk = lax.broadcasted_iota(jnp.int32, (rows, nbp), 1)
    for kv in range(MOBA_KV_HEADS):
        cols = slice(kv * LANES, (kv + 1) * LANES)
        qf = jnp.concatenate(
            [q_ref[:, (kv * group + g) * LANES:(kv * group + g + 1) * LANES] for g in range(group)], axis=0)
        gate = lax.dot_general(qf, mean_ref[:, cols], (((1,), (1,)), ((), ())),
                               precision=lax.Precision.HIGHEST, preferred_element_type=F32)
        past = blk < i
        sel = jnp.logical_and(_topk_mask(jnp.where(past, gate, NEG_INF), min(MOBA_TOPK, nblocks)), past)
        q = jnp.concatenate([qf * scale, jnp.where(sel, 0.0, NEG_INF)], axis=-1).astype(BF16)
        _flash_init(m_ref, l_ref, acc_ref)

        def logits(j, w, masked=True, q=q, cols=cols):
            k = k_ref[pl.ds(pl.multiple_of(j * t, t), w * t), cols]
            kblk = j + lax.broadcasted_iota(jnp.int32, (w * t, nbp), 0) // t
            hot = jnp.logical_and(lax.broadcasted_iota(jnp.int32, (w * t, nbp), 1) == kblk, masked)
            return _dot_nt(jnp.concatenate([k, jnp.where(hot, 1.0, 0.0).astype(BF16)], axis=-1), q)

        def update(st, j, w, cols=cols):
            _flash_update_t(st, _vt_tiles(vt_ref, j, w, cols), m_ref, l_ref, acc_ref)

        _far_tiles(jnp.maximum(i - 1, 0), logits, update, st0_ref, st1_ref)

        @pl.when(i >= 1)
        def _(logits=logits, update=update, kv=kv):
            update(logits(i - 1, 1) + bias_ref[kv, 1], i - 1, 1)

        update(logits(i, 1, masked=False) + bias_ref[kv, 0], i, 1)
        o = (acc_ref[...] / l_ref[...]).T
        for g in range(group):
            hd = kv * group + g
            o_ref[:, hd * LANES:(hd + 1) * LANES] = o[g * t:(g + 1) * t]


def _moba_prompt(q, means, kb, vt, bias, nblocks):
    s = kb.shape[0]
    t = ATT_TILE
    rows = (MOBA_HEADS // MOBA_KV_HEADS) * t
    return pl.pallas_call(
        functools.partial(_moba_prompt_kernel, nblocks=nblocks),
        out_shape=jax.ShapeDtypeStruct((s, MOBA_WIDTH), F32),
        grid=(s // t,),
        in_specs=[pl.BlockSpec((t, MOBA_WIDTH), lambda i: (i, 0)), _resident(means.shape),
                  _resident(kb.shape), _resident(vt.shape), _resident(bias.shape)],
        out_specs=pl.BlockSpec((t, MOBA_WIDTH), lambda i: (i, 0)),
        scratch_shapes=[pltpu.VMEM((1, rows), F32), pltpu.VMEM((1, rows), F32),
                        pltpu.VMEM((LANES, rows), F32),
                        pltpu.VMEM((2 * t, rows), F32), pltpu.VMEM((2 * t, rows), F32)],
        compiler_params=_params(1),
        name="moba_prompt",
    )(q, means, kb, vt, bias)


def _final_kernel(x_ref, o_ref, g_ref, wout_ref, fg_ref, y_ref):
    mix = (o_ref[...] * _silu(g_ref[...])).astype(BF16)
    x2 = x_ref[...] + _dot(mix, wout_ref[...])
    y_ref[...] = _rms(x2, fg_ref[...], NORM_EPS)


def _final(x1, o, g, wout, fg):
    n, d = x1.shape
    tm = min(ROW_TILE, n)
    row = lambda w: pl.BlockSpec((tm, w), lambda i: (i, 0))
    return pl.pallas_call(
        _final_kernel,
        out_shape=jax.ShapeDtypeStruct((n, d), F32),
        grid=(n // tm,),
        in_specs=[row(d), row(MOBA_WIDTH), row(MOBA_WIDTH), _full(wout.shape), _full(fg.shape)],
        out_specs=row(d),
        compiler_params=_params(1),
        name="odd_out_final",
    )(x1, o, g, wout, fg)


def _fetch_pages(pt_ref, caches, bufs, sems, n):
    b, c = pl.program_id(0), pl.program_id(1)
    steps = pl.num_programs(1)
    step = b * steps + c
    slot = lax.rem(step, 2)

    def copies(sb, sc, sslot, lookup):
        for p in range(n):
            page = pt_ref[sb, sc * n + p] if lookup else 0
            for cache, buf, sem in zip(caches, bufs, sems):
                yield pltpu.make_async_copy(cache.at[page], buf.at[sslot, p], sem.at[sslot])

    @pl.when(step == 0)
    def _():
        for cp in copies(b, c, slot, True):
            cp.start()

    @pl.when(step + 1 < pl.num_programs(0) * steps)
    def _():
        wrap = c == steps - 1
        for cp in copies(jnp.where(wrap, b + 1, b), jnp.where(wrap, 0, c + 1), 1 - slot, True):
            cp.start()

    for cp in copies(b, c, slot, False):
        cp.wait()
    return slot


def _page_scratch(caches, n):
    bufs = [pltpu.VMEM((2, n) + c.shape[1:], c.dtype) for c in caches]
    return bufs + [pltpu.SemaphoreType.DMA((2,)) for _ in caches]


def _hbm_specs(caches):
    return [pl.BlockSpec(memory_space=pl.ANY) for _ in caches]


def _seq_spec(shape):
    return pl.BlockSpec((1,) + shape, lambda b, c, pt: (b,) + (0,) * len(shape))


def _const_spec(a):
    return pl.BlockSpec(a.shape, lambda b, c, pt: (0,) * a.ndim)


def _mla_sample_kernel(pt_ref, q_ref, knew_ref, mask_ref, ckv_hbm, krt_hbm, o_ref, m_ref, l_ref, acc_ref,
                       ckv_buf, krt_buf, ckv_sem, krt_sem, *, n):
    slot = _fetch_pages(pt_ref, (ckv_hbm, krt_hbm), (ckv_buf, krt_buf), (ckv_sem, krt_sem), n)
    c = pl.program_id(1)
    q = q_ref[0]

    @pl.when(c == 0)
    def _():
        _flash_init(m_ref, l_ref, acc_ref)

    ckv = ckv_buf[slot].reshape(n * PAGE_SIZE, MLA_KV_LORA).astype(BF16)
    krt = jnp.concatenate([krt_buf[slot, p].astype(BF16) for p in range(n)], axis=1)
    s = _dot_nt(q[:, :MLA_KV_LORA], ckv) + _dot(q[:, MLA_KV_LORA:MLA_KV_LORA + MLA_ROPE], krt)
    _flash_update(s, ckv, m_ref, l_ref, acc_ref)

    @pl.when(c == pl.num_programs(1) - 1)
    def _():
        knew = knew_ref[0]
        _flash_update(_dot_nt(q, knew) + mask_ref[...], knew[:, :MLA_KV_LORA], m_ref, l_ref, acc_ref)
        o_ref[0] = acc_ref[...] / l_ref[...]


def _mla_sample(page_table, q, knew, mask, cache_ckv, cache_krt):
    nseq, rows, _ = q.shape
    n = min(PAGES_PER_STEP, page_table.shape[1])
    steps = page_table.shape[1] // n
    caches = (cache_ckv, cache_krt)
    grid_spec = pltpu.PrefetchScalarGridSpec(
        num_scalar_prefetch=1,
        grid=(nseq, steps),
        in_specs=[_seq_spec((rows, 2 * LANES)), _seq_spec((LANES, 2 * LANES)), _const_spec(mask)]
        + _hbm_specs(caches),
        out_specs=_seq_spec((rows, MLA_KV_LORA)),
        scratch_shapes=[pltpu.VMEM((rows, 1), F32), pltpu.VMEM((rows, 1), F32),
                        pltpu.VMEM((rows, MLA_KV_LORA), F32)] + _page_scratch(caches, n),
    )
    return pl.pallas_call(
        functools.partial(_mla_sample_kernel, n=n),
        out_shape=jax.ShapeDtypeStruct((nseq, rows, MLA_KV_LORA), F32),
        grid_spec=grid_spec,
        compiler_params=_params(2),
        name="mla_sample",
    )(page_table, q, knew, mask, *caches)


def _diff_sample_kernel(pt_ref, q_ref, knew_ref, vnew_ref, pmask_ref, blast_ref, bnew_ref, lp_ref, k_hbm, v_hbm,
                        o_ref, m_ref, l_ref, acc_ref, k_buf, v_buf, k_sem, v_sem, *, n, lambda_init):
    slot = _fetch_pages(pt_ref, (k_hbm, v_hbm), (k_buf, v_buf), (k_sem, v_sem), n)
    c = pl.program_id(1)
    last = pl.num_programs(1) - 1
    q = q_ref[0]
    step_rows = n * k_buf.shape[2]

    @pl.when(c == 0)
    def _():
        _flash_init(m_ref, l_ref, acc_ref)

    def pages(bias_ref):
        k = k_buf[slot].reshape(step_rows, LANES).astype(BF16)
        v = v_buf[slot].reshape(step_rows, LANES).astype(BF16)
        _flash_update(_dot_nt(q, k) + bias_ref[...], v, m_ref, l_ref, acc_ref)

    @pl.when(c != last)
    def _():
        pages(pmask_ref)

    @pl.when(c == last)
    def _():
        pages(blast_ref)
        _flash_update(_dot_nt(q, knew_ref[0].astype(BF16)) + bnew_ref[...], vnew_ref[0].astype(BF16),
                      m_ref, l_ref, acc_ref)
        lam = _diff_lambda(lp_ref[...], lambda_init)
        o = acc_ref[...] / l_ref[...]
        per_kv = q.shape[0] // DIFF_KV_HEADS
        half = per_kv // 2
        for kv in range(DIFF_KV_HEADS):
            r0 = kv * per_kv
            o_ref[0, kv] = o[r0:r0 + half] - lam * o[r0 + half:r0 + per_kv]


def _diff_sample(page_table, q, knew, vnew, pmask, blast, bnew, lp, cache_k, cache_v, lambda_init):
    nseq, rows, _ = q.shape
    n = min(PAGES_PER_STEP, page_table.shape[1])
    steps = page_table.shape[1] // n
    prow = cache_k.shape[1]
    caches = (cache_k, cache_v)
    grid_spec = pltpu.PrefetchScalarGridSpec(
        num_scalar_prefetch=1,
        grid=(nseq, steps),
        in_specs=[_seq_spec((rows, LANES)), _seq_spec((prow, LANES)), _seq_spec((prow, LANES)),
                  _const_spec(pmask), _const_spec(blast), _const_spec(bnew), _const_spec(lp)]
        + _hbm_specs(caches),
        out_specs=_seq_spec((DIFF_KV_HEADS, rows // 4, LANES)),
        scratch_shapes=[pltpu.VMEM((rows, 1), F32), pltpu.VMEM((rows, 1), F32), pltpu.VMEM((rows, LANES), F32)]
        + _page_scratch(caches, n),
    )
    return pl.pallas_call(
        functools.partial(_diff_sample_kernel, n=n, lambda_init=lambda_init),
        out_shape=jax.ShapeDtypeStruct((nseq, DIFF_KV_HEADS, rows // 4, LANES), F32),
        grid_spec=grid_spec,
        compiler_params=_params(2),
        name="diff_sample",
    )(page_table, q, knew, vnew, pmask, blast, bnew, lp, *caches)


def _moba_sample_kernel(pt_ref, q_ref, knew_ref, vnew_ref, pmask_ref, blast_ref, bnew_ref, k_hbm, v_hbm,
                        o_ref, gate_ref, mb_ref, lb_ref, ob_ref, k_buf, v_buf, k_sem, v_sem, *, n, nblocks):
    slot = _fetch_pages(pt_ref, (k_hbm, v_hbm), (k_buf, v_buf), (k_sem, v_sem), n)
    c = pl.program_id(1)
    last = pl.num_programs(1) - 1
    ppb = MOBA_BLOCK // PAGE_SIZE
    bps = n // ppb
    prow = k_buf.shape[2]
    brow = ppb * prow
    rows = q_ref.shape[1]
    qf = q_ref[0]
    qb = (qf * (MOBA_HEAD_DIM ** -0.5 * LOG2E)).astype(BF16)
    lane = lax.broadcasted_iota(jnp.int32, (rows, nblocks), 1)
    row_kv = lax.broadcasted_iota(jnp.int32, (rows, LANES), 0) // (rows // MOBA_KV_HEADS)
    key_kv = lax.broadcasted_iota(jnp.int32, (brow, LANES), 0) % MOBA_KV_HEADS

    @pl.when(c == 0)
    def _():
        for ref in (gate_ref, mb_ref, lb_ref):
            ref[...] = jnp.zeros(ref.shape, F32)

    def put(ref, blk, col):
        ref[...] = jnp.where(lane == blk, col, ref[...])

    def blocks(last_bias_ref):
        kf = k_buf[slot].reshape(n * prow, LANES)
        vb = v_buf[slot].reshape(n * prow, LANES).astype(BF16)
        s_all = _dot_nt(qb, kf.astype(BF16))
        for bi in range(bps):
            blk = c * bps + bi
            kblk = kf[bi * brow:(bi + 1) * brow]
            kmean = jnp.zeros((rows, LANES), F32)
            for kv in range(MOBA_KV_HEADS):
                mean_kv = jnp.sum(jnp.where(key_kv == kv, kblk, 0.0), axis=0, keepdims=True) * (1.0 / MOBA_BLOCK)
                kmean = jnp.where(row_kv == kv, mean_kv, kmean)
            put(gate_ref, blk, jnp.sum(qf * kmean, axis=-1, keepdims=True))
            bias_ref = last_bias_ref if bi == bps - 1 else pmask_ref
            s = s_all[:, bi * brow:(bi + 1) * brow] + bias_ref[...]
            m = jnp.max(s, axis=-1, keepdims=True)
            p = jnp.exp2(s - m)
            put(mb_ref, blk, m)
            put(lb_ref, blk, jnp.sum(p, axis=-1, keepdims=True))
            ob_ref[blk] = _dot(p.astype(BF16), vb[bi * brow:(bi + 1) * brow])

    @pl.when(c != last)
    def _():
        blocks(pmask_ref)

    @pl.when(c == last)
    def _():
        blocks(blast_ref)
        s = _dot_nt(qb, knew_ref[0].astype(BF16)) + bnew_ref[...]
        m_new = jnp.max(s, axis=-1, keepdims=True)
        p = jnp.exp2(s - m_new)
        l_new = jnp.sum(p, axis=-1, keepdims=True)
        o_new = _dot(p.astype(BF16), vnew_ref[0].astype(BF16))
        sel = _topk_mask(gate_ref[...], min(MOBA_TOPK, nblocks))
        mb = mb_ref[...]
        m_all = jnp.maximum(jnp.max(jnp.where(sel, mb, -jnp.inf), axis=-1, keepdims=True), m_new)
        w = jnp.where(sel, jnp.exp2(mb - m_all), 0.0)
        w_new = jnp.exp2(m_new - m_all)
        den = jnp.sum(w * lb_ref[...], axis=-1, keepdims=True) + w_new * l_new
        num = w_new * o_new
        for blk in range(nblocks):
            num = num + w[:, blk:blk + 1] * ob_ref[blk]
        o_ref[0] = num / den


def _moba_sample(page_table, q, knew, vnew, pmask, blast, bnew, cache_k, cache_v):
    nseq, rows, _ = q.shape
    ppb = MOBA_BLOCK // PAGE_SIZE
    nblocks = page_table.shape[1] // ppb
    n = min(PAGES_PER_STEP, page_table.shape[1])
    steps = page_table.shape[1] // n
    prow = cache_k.shape[1]
    caches = (cache_k, cache_v)
    grid_spec = pltpu.PrefetchScalarGridSpec(
        num_scalar_prefetch=1,
        grid=(nseq, steps),
        in_specs=[_seq_spec((rows, LANES)), _seq_spec((prow, LANES)), _seq_spec((prow, LANES)),
                  _const_spec(pmask), _const_spec(blast), _const_spec(bnew)]
        + _hbm_specs(caches),
        out_specs=_seq_spec((rows, LANES)),
        scratch_shapes=[pltpu.VMEM((rows, nblocks), F32), pltpu.VMEM((rows, nblocks), F32),
                        pltpu.VMEM((rows, nblocks), F32), pltpu.VMEM((nblocks, rows, LANES), F32)]
        + _page_scratch(caches, n),
    )
    return pl.pallas_call(
        functools.partial(_moba_sample_kernel, n=n, nblocks=nblocks),
        out_shape=jax.ShapeDtypeStruct((nseq, rows, LANES), F32),
        grid_spec=grid_spec,
        compiler_params=_params(2),
        name="moba_sample",
    )(page_table, q, knew, vnew, pmask, blast, bnew, *caches)


def _rope_tables(pos):
    half = MLA_ROPE // 2
    inv = ROPE_BASE ** (-jnp.arange(half, dtype=F32) / half)
    ang = pos.astype(F32)[:, None] * inv
    pad = jnp.zeros((pos.shape[0], LANES - 2 * half), F32)
    cos = jnp.cos(ang)
    sin = jnp.sin(ang)
    return jnp.concatenate([cos, cos, pad], axis=-1), jnp.concatenate([sin, sin, pad], axis=-1)


def _lane_pad(w, width):
    return jnp.pad(w, ((0, 0), (0, width - w.shape[1])))


def _even_weights(w_in, w_uq, w_uk, w_uv):
    half = MLA_ROPE // 2
    sizes = [MLA_Q_LORA, MLA_KV_LORA, MLA_ROPE, MLA_WIDTH, DIFF_WIDTH, DIFF_KV_WIDTH, DIFF_KV_WIDTH]
    cq, ckv, kr, gm, dq, dk, dv, gd = jnp.split(w_in, np.cumsum(sizes), axis=1)
    ka = _lane_pad(kr, LANES)
    kb = _lane_pad(jnp.concatenate([-kr[:, half:], kr[:, :half]], axis=1), LANES)
    win = jnp.concatenate([gm, dq, dk, dv, gd, ckv, _lane_pad(cq, 2 * LANES), ka, kb], axis=1).astype(BF16)
    nope = jnp.concatenate([_lane_pad(w_uq[:, h, :MLA_NOPE], LANES) for h in range(MLA_HEADS)], axis=1)
    ra = jnp.concatenate([_lane_pad(w_uq[:, h, MLA_NOPE:], LANES) for h in range(MLA_HEADS)], axis=1)
    rb = jnp.concatenate(
        [_lane_pad(jnp.concatenate([-w_uq[:, h, MLA_NOPE + half:], w_uq[:, h, MLA_NOPE:MLA_NOPE + half]], axis=1),
                   LANES) for h in range(MLA_HEADS)], axis=1)
    wuq = jnp.concatenate([nope, ra, rb], axis=1).astype(BF16)
    wukt = jnp.pad(jnp.transpose(w_uk, (1, 2, 0)), ((0, 0), (0, LANES - MLA_NOPE), (0, 0))).astype(BF16)
    wuv = jnp.transpose(w_uv, (1, 0, 2)).astype(BF16)
    return win, wuq, wukt, wuv


def _interleave_kv(tile):
    nkv, r, j = tile.shape
    own = jnp.eye(nkv, dtype=jnp.bool_)[:, None, None, :]
    return jnp.where(own, tile[..., None], NEG_INF).reshape(nkv * r, j * nkv)


def kernel(x_prompt, x_sample, cache_mla_ckv, cache_mla_krope, cache_diff_k, cache_diff_v, cache_moba_k, cache_moba_v, page_table, norm_g, final_norm_g, rel_bias, w_in_even, mla_q_norm_g, mla_w_uq, mla_kv_norm_g, mla_w_uk, mla_w_uv, diff_lambda, diff_subln_g, w_out_even, w_in_odd, w_out_odd):
    _, s_len, d_model = x_prompt.shape
    n_dec, t_len, _ = x_sample.shape
    n_pages = page_table.shape[1]
    past_len = n_pages * PAGE_SIZE
    n_smp = n_dec * t_len
    t = ATT_TILE
    n_step = min(PAGES_PER_STEP, n_pages)
    assert norm_g.shape[0] == 2 and x_prompt.shape[0] == 1
    assert ROW_TILE == MOBA_BLOCK == ATT_TILE
    assert s_len % t == 0 and n_smp % min(ROW_TILE, n_smp) == 0 and past_len % MOBA_BLOCK == 0
    assert n_pages % n_step == 0 and n_step % (MOBA_BLOCK // PAGE_SIZE) == 0 and t_len <= 8
    assert FAR_DIST <= PAGE_SIZE + 1 and FAR_DIST <= t + 1
    dgroup = DIFF_HEADS // DIFF_KV_HEADS
    mgroup = MOBA_HEADS // MOBA_KV_HEADS
    nblocks_p = s_len // MOBA_BLOCK
    lambda_init = 0.8 - 0.6 * math.exp(-0.3 * 0)

    tab = jnp.pad(rel_bias.astype(F32), ((0, 0), (0, 8)))
    zero_col = rel_bias.shape[1]
    band0 = _toeplitz_bias(tab, t, t, 0, True, transposed=True)
    band1 = _toeplitz_bias(tab, t, t, t, False, transposed=True)
    s_last_page = _toeplitz_bias(tab, 8, PAGE_SIZE, PAGE_SIZE, False)
    s_last_block = _toeplitz_bias(tab, 8, MOBA_BLOCK, MOBA_BLOCK, False)
    s_new = _toeplitz_bias(tab, 8, LANES, 0, True)
    diff_cols = [[m * DIFF_HEADS + kv * dgroup + g for m in range(2) for g in range(dgroup)]
                 for kv in range(DIFF_KV_HEADS)]
    moba_cols = [[kv * mgroup + g for g in range(mgroup)] for kv in range(MOBA_KV_HEADS)]
    lanes_of = lambda tb, cols: jnp.stack([jnp.concatenate([tb[c] for c in cl], axis=1) for cl in cols])
    rows_of = lambda tb, cols: jnp.stack([jnp.concatenate([tb[c, :t_len] for c in cl], axis=0) for cl in cols])
    diff_band = jnp.stack([lanes_of(band0, diff_cols), lanes_of(band1, diff_cols)], axis=1)
    moba_band = jnp.stack([lanes_of(band0, moba_cols), lanes_of(band1, moba_cols)], axis=1)
    mla_mask = jnp.tile(band0[zero_col], (1, MLA_HEADS))
    mla_new_mask = jnp.tile(s_new[zero_col, :t_len], (MLA_HEADS, 1))
    drows = 2 * dgroup * t_len
    diff_pmask = _interleave_kv(jnp.zeros((DIFF_KV_HEADS, drows, n_step * PAGE_SIZE), F32))
    diff_last = jnp.concatenate([diff_pmask[:, :(n_step - 1) * 2 * PAGE_SIZE],
                                 _interleave_kv(rows_of(s_last_page, diff_cols))], axis=1)
    diff_new = _interleave_kv(rows_of(s_new, diff_cols))
    moba_pmask = _interleave_kv(jnp.zeros((MOBA_KV_HEADS, mgroup * t_len, MOBA_BLOCK), F32))
    moba_last = _interleave_kv(rows_of(s_last_block, moba_cols))
    moba_new = _interleave_kv(rows_of(s_new, moba_cols))

    win_e, wuq, wukt, wuv = _even_weights(w_in_even[0], mla_w_uq[0], mla_w_uk[0], mla_w_uv[0])
    wout_e = w_out_even[0].astype(BF16)
    win_o = w_in_odd[0].astype(BF16)
    wout_o = w_out_odd[0].astype(BF16)
    ng0, ng1, fg = norm_g[0][None], norm_g[1][None], final_norm_g[None]
    qg, kvg, subg = mla_q_norm_g[0][None], mla_kv_norm_g[0][None], diff_subln_g[0][None]
    lp = diff_lambda[0].astype(F32)

    xp = x_prompt[0]
    xs = x_sample.reshape(n_smp, d_model)
    cos_p, sin_p = _rope_tables(jnp.arange(s_len))
    cos_s, sin_s = _rope_tables(jnp.tile(past_len + jnp.arange(t_len), n_dec))

    krt_cache = jnp.swapaxes(cache_mla_krope, 2, 3)
    kv_rows = lambda a: a[0].reshape(a.shape[1], a.shape[2] * a.shape[3], a.shape[4])
    new_rows = lambda a, nkv: jnp.pad(a.reshape(n_dec, t_len * nkv, LANES),
                                      ((0, 0), (0, (PAGE_SIZE - t_len) * nkv), (0, 0)))

    (qm_p, kcat_p, ckvt_p, ckv_p, kr_p, gm_p, dq_p, dk_p, dkb_p, dv_p, dvt_p, gd_p) = _even_in(
        xp, ng0, win_e, qg, wuq, wukt, kvg, cos_p, sin_p)
    olat_p = _mla_prompt(qm_p, kcat_p, ckvt_p, mla_mask)
    diffo_p = _diff_prompt(dq_p, dkb_p, dvt_p, diff_band, lp, lambda_init)
    (x1_p, q_p, k_p, kb_p, v_p, vt_p, g_p, means_p) = _mid(
        xp, olat_p, wuv, diffo_p, gm_p, gd_p, subg, wout_e, ng1, win_o, 1.0 - lambda_init)

    (qm_s, kcat_s, _, ckv_s, kr_s, gm_s, dq_s, dk_s, _, dv_s, _, gd_s) = _even_in(
        xs, ng0, win_e, qg, wuq, wukt, kvg, cos_s, sin_s)
    q_mla = qm_s.reshape(MLA_HEADS, n_dec, t_len, 2 * LANES).transpose(1, 0, 2, 3).reshape(
        n_dec, MLA_HEADS * t_len, 2 * LANES)
    knew_mla = jnp.pad(kcat_s.reshape(n_dec, t_len, 2 * LANES), ((0, 0), (0, LANES - t_len), (0, 0)))
    olat_s = _mla_sample(page_table, q_mla, knew_mla, mla_new_mask, cache_mla_ckv[0], krt_cache[0])
    olat_s = olat_s.reshape(n_dec, MLA_HEADS, t_len, MLA_KV_LORA).transpose(1, 0, 2, 3).reshape(
        MLA_HEADS, n_smp, MLA_KV_LORA)
    q_diff = dq_s.reshape(DIFF_KV_HEADS, 4, n_dec, t_len, LANES).transpose(2, 0, 1, 3, 4).reshape(
        n_dec, DIFF_KV_HEADS * drows, LANES)
    diffo_s = _diff_sample(page_table, q_diff, new_rows(dk_s, DIFF_KV_HEADS), new_rows(dv_s, DIFF_KV_HEADS),
                           diff_pmask, diff_last, diff_new, lp, kv_rows(cache_diff_k), kv_rows(cache_diff_v),
                           lambda_init)
    diffo_s = diffo_s.reshape(n_dec, DIFF_KV_HEADS, dgroup, t_len, LANES).transpose(0, 3, 1, 2, 4).reshape(
        n_smp, DIFF_WIDTH)
    (x1_s, q_s, k_s, _, v_s, _, g_s, _) = _mid(
        xs, olat_s, wuv, diffo_s, gm_s, gd_s, subg, wout_e, ng1, win_o, 1.0 - lambda_init)

    means = means_p.reshape(nblocks_p, MOBA_KV_WIDTH)
    means = jnp.pad(means, ((0, -nblocks_p % LANES), (0, 0)))
    o_p = _moba_prompt(q_p, means, kb_p, vt_p, moba_band, nblocks_p)
    y_p = _final(x1_p, o_p, g_p, wout_o, fg)

    q_moba = q_s.reshape(n_dec, t_len, MOBA_KV_HEADS, mgroup, LANES).transpose(0, 2, 3, 1, 4).reshape(
        n_dec, MOBA_HEADS * t_len, LANES)
    o_s = _moba_sample(page_table, q_moba, new_rows(k_s, MOBA_KV_HEADS), new_rows(v_s, MOBA_KV_HEADS),
                       moba_pmask, moba_last, moba_new, kv_rows(cache_moba_k), kv_rows(cache_moba_v))
    o_s = o_s.reshape(n_dec, MOBA_KV_HEADS, mgroup, t_len, LANES).transpose(0, 3, 1, 2, 4).reshape(
        n_smp, MOBA_WIDTH)
    y_s = _final(x1_s, o_s, g_s, wout_o, fg)

    kv4 = lambda a, b, s: a.reshape(1, b, s, 2, LANES)
    return (y_p[None], y_s.reshape(n_dec, t_len, d_model),
            ckv_p.reshape(1, 1, s_len, MLA_KV_LORA), kr_p.reshape(1, 1, s_len, MLA_ROPE),
            kv4(dk_p, 1, s_len), kv4(dv_p, 1, s_len), kv4(k_p, 1, s_len), kv4(v_p, 1, s_len),
            ckv_s.reshape(1, n_dec, t_len, MLA_KV_LORA), kr_s.reshape(1, n_dec, t_len, MLA_ROPE),
            kv4(dk_s, n_dec, t_len), kv4(dv_s, n_dec, t_len), kv4(k_s, n_dec, t_len), kv4(v_s, n_dec, t_len))
```

```python
import functools
import math

import jax
import jax.numpy as jnp
import numpy as np
from jax import lax
from jax.experimental import pallas as pl
from jax.experimental.pallas import tpu as pltpu

F32 = jnp.float32
BF16 = jnp.bfloat16

PAGE_SIZE = 128
MLA_HEADS = 8
MLA_Q_LORA = 192
MLA_KV_LORA = 128
MLA_NOPE = 64
MLA_ROPE = 32
MLA_V = 64
ROPE_BASE = 10000.0
DIFF_HEADS = 4
DIFF_KV_HEADS = 2
DIFF_HEAD_DIM = 64
SUBLN_EPS = 1e-5
MOBA_HEADS = 8
MOBA_KV_HEADS = 2
MOBA_HEAD_DIM = 128
MOBA_BLOCK = 256
MOBA_TOPK = 3
NUM_BUCKETS = 32
REL_MAX_DISTANCE = 128
REL_MAX_EXACT = NUM_BUCKETS // 2
NORM_EPS = 1e-6
NEG_INF = -1e30
LOG2E = math.log2(math.e)

LANES = 128
ROW_TILE = 256
ATT_TILE = 256
PAGES_PER_STEP = 16
MLA_PAGES_PER_STEP = 32
VMEM_LIMIT = 56 * 1024 * 1024

MLA_WIDTH = MLA_HEADS * MLA_V
DIFF_WIDTH = DIFF_HEADS * 2 * DIFF_HEAD_DIM
DIFF_KV_WIDTH = DIFF_KV_HEADS * 2 * DIFF_HEAD_DIM
MOBA_WIDTH = MOBA_HEADS * MOBA_HEAD_DIM
MOBA_KV_WIDTH = MOBA_KV_HEADS * MOBA_HEAD_DIM


def _bucket_upper_bounds():
    d = np.arange(0, 4 * REL_MAX_DISTANCE)
    v = np.log(np.maximum(d, 1) / REL_MAX_EXACT) / math.log(REL_MAX_DISTANCE / REL_MAX_EXACT)
    v = v * (NUM_BUCKETS - REL_MAX_EXACT)
    frac = np.abs(v[REL_MAX_EXACT + 1:REL_MAX_DISTANCE] - np.round(v[REL_MAX_EXACT + 1:REL_MAX_DISTANCE]))
    assert frac.min() > 1e-3
    b = np.where(d < REL_MAX_EXACT, d, np.minimum(REL_MAX_EXACT + np.floor(v).astype(np.int64), NUM_BUCKETS - 1))
    hi = [int(d[b == k].max()) if np.any(b == k) else None for k in range(NUM_BUCKETS)]
    far = int(hi[NUM_BUCKETS - 2]) + 1
    return hi, far


BUCKET_HI, FAR_DIST = _bucket_upper_bounds()


def _dot(a, b):
    return jnp.dot(a, b, preferred_element_type=F32)


def _dot_nt(a, b):
    return lax.dot_general(a, b, (((1,), (1,)), ((), ())), preferred_element_type=F32)


def _rms(x, g, eps):
    return x * lax.rsqrt(jnp.mean(x * x, axis=-1, keepdims=True) + eps) * g


def _silu(g):
    return g / (1.0 + jnp.exp(-g))


def _full(shape):
    nd = len(shape)
    return pl.BlockSpec(shape, lambda *_: (0,) * nd)


def _resident(shape):
    nd = len(shape)
    return pl.BlockSpec(shape, lambda *_: (0,) * nd, pipeline_mode=pl.Buffered(1))


def _params(n_axes):
    return pltpu.CompilerParams(dimension_semantics=("arbitrary",) * n_axes, vmem_limit_bytes=VMEM_LIMIT)


def _toeplitz_kernel(tab_ref, out_ref, *, off, causal, transposed):
    c = pl.program_id(0)
    shape = out_ref.shape[1:]
    r = lax.broadcasted_iota(jnp.int32, shape, 0)
    s = lax.broadcasted_iota(jnp.int32, shape, 1)
    q, k = (s, r) if transposed else (r, s)
    dist = off + q - k
    acc = jnp.zeros(shape, F32) + tab_ref[NUM_BUCKETS - 1, c]
    for b in range(NUM_BUCKETS - 2, -1, -1):
        if BUCKET_HI[b] is not None:
            acc = jnp.where(dist <= BUCKET_HI[b], tab_ref[b, c], acc)
    acc = (acc - tab_ref[NUM_BUCKETS - 1, c]) * LOG2E
    if causal:
        acc = jnp.where(k <= q, acc, NEG_INF)
    out_ref[0] = acc


def _toeplitz_bias(tab_pad, rows, cols, off, causal, transposed=False):
    n = tab_pad.shape[1]
    return pl.pallas_call(
        functools.partial(_toeplitz_kernel, off=off, causal=causal, transposed=transposed),
        out_shape=jax.ShapeDtypeStruct((n, rows, cols), F32),
        grid=(n,),
        in_specs=[pl.BlockSpec(memory_space=pltpu.SMEM)],
        out_specs=pl.BlockSpec((1, rows, cols), lambda c: (c, 0, 0)),
        compiler_params=_params(1),
        name="toeplitz_bias",
    )(tab_pad)


_E_GM = 0
_E_DQ = _E_GM + MLA_WIDTH
_E_DK = _E_DQ + DIFF_WIDTH
_E_DV = _E_DK + DIFF_KV_WIDTH
_E_GD = _E_DV + DIFF_KV_WIDTH
_E_CKV = _E_GD + DIFF_WIDTH
_E_CQ = _E_CKV + MLA_KV_LORA
_E_KA = _E_CQ + 2 * LANES
_E_KB = _E_KA + LANES
_E_END = _E_KB + LANES


def _even_in_kernel(x_ref, ng_ref, win_ref, qg_ref, wuq_ref, wukt_ref, kvg_ref, cos_ref, sin_ref,
                    qm_ref, kcat_ref, ckvt_ref, ckv_ref, kr_ref, gm_ref, dq_ref, dk_ref, dkb_ref, dv_ref,
                    dvt_ref, gd_ref):
    h = _rms(x_ref[...], ng_ref[...], NORM_EPS).astype(BF16)
    z = _dot(h, win_ref[...])
    cos2 = cos_ref[...]
    sin2 = sin_ref[...]
    gm_ref[...] = z[:, _E_GM:_E_DQ]
    gd_ref[...] = z[:, _E_GD:_E_CKV]
    dk = z[:, _E_DK:_E_DV]
    dv = z[:, _E_DV:_E_GD]
    dk_ref[...] = dk
    dkb_ref[...] = dk.astype(BF16)
    dv_ref[...] = dv
    dvt_ref[0] = dv.T.astype(BF16)
    lane = lax.broadcasted_iota(jnp.int32, (x_ref.shape[0], LANES), 1)
    dscale = DIFF_HEAD_DIM ** -0.5 * LOG2E
    group = DIFF_HEADS // DIFF_KV_HEADS
    for kv in range(DIFF_KV_HEADS):
        for g in range(group):
            c0 = _E_DQ + (kv * group + g) * 2 * DIFF_HEAD_DIM
            src = z[:, c0:c0 + 2 * DIFF_HEAD_DIM] * dscale
            dq_ref[kv, g] = jnp.where(lane < DIFF_HEAD_DIM, src, 0.0).astype(BF16)
            dq_ref[kv, group + g] = jnp.where(lane >= DIFF_HEAD_DIM, src, 0.0).astype(BF16)
    ckv = _rms(z[:, _E_CKV:_E_CQ], kvg_ref[...], NORM_EPS)
    kr = z[:, _E_KA:_E_KB] * cos2 + z[:, _E_KB:_E_END] * sin2
    ckv_ref[...] = ckv
    kr_ref[...] = kr[:, :MLA_ROPE]
    kcat_ref[...] = jnp.concatenate([ckv, kr], axis=-1).astype(BF16)
    ckvt_ref[0] = ckv.T.astype(BF16)
    cq = _rms(z[:, _E_CQ:_E_CQ + MLA_Q_LORA], qg_ref[...], NORM_EPS).astype(BF16)
    qall = _dot(cq, wuq_ref[...])
    qscale = (MLA_NOPE + MLA_ROPE) ** -0.5 * LOG2E
    nh = MLA_HEADS * LANES
    for hd in range(MLA_HEADS):
        qn = qall[:, hd * LANES:(hd + 1) * LANES].astype(BF16)
        ql = _dot(qn, wukt_ref[hd])
        qr = (qall[:, nh + hd * LANES:nh + (hd + 1) * LANES] * cos2
              + qall[:, 2 * nh + hd * LANES:2 * nh + (hd + 1) * LANES] * sin2)
        qm_ref[hd] = (jnp.concatenate([ql, qr], axis=-1) * qscale).astype(BF16)


def _even_in(x, ng, win, qg, wuq, wukt, kvg, cos2, sin2):
    n, d = x.shape
    tm = min(ROW_TILE, n)
    row = lambda w: pl.BlockSpec((tm, w), lambda i: (i, 0))
    tmajor = lambda w: pl.BlockSpec((1, w, tm), lambda i: (i, 0, 0))
    out_shape = (
        jax.ShapeDtypeStruct((MLA_HEADS, n, 2 * LANES), BF16),
        jax.ShapeDtypeStruct((n, 2 * LANES), BF16),
        jax.ShapeDtypeStruct((n // tm, MLA_KV_LORA, tm), BF16),
        jax.ShapeDtypeStruct((n, MLA_KV_LORA), F32),
        jax.ShapeDtypeStruct((n, MLA_ROPE), F32),
        jax.ShapeDtypeStruct((n, MLA_WIDTH), F32),
        jax.ShapeDtypeStruct((DIFF_KV_HEADS, 4, n, LANES), BF16),
        jax.ShapeDtypeStruct((n, DIFF_KV_WIDTH), F32),
        jax.ShapeDtypeStruct((n, DIFF_KV_WIDTH), BF16),
        jax.ShapeDtypeStruct((n, DIFF_KV_WIDTH), F32),
        jax.ShapeDtypeStruct((n // tm, DIFF_KV_WIDTH, tm), BF16),
        jax.ShapeDtypeStruct((n, DIFF_WIDTH), F32),
    )
    out_specs = (
        pl.BlockSpec((MLA_HEADS, tm, 2 * LANES), lambda i: (0, i, 0)),
        row(2 * LANES), tmajor(MLA_KV_LORA), row(MLA_KV_LORA), row(MLA_ROPE), row(MLA_WIDTH),
        pl.BlockSpec((DIFF_KV_HEADS, 4, tm, LANES), lambda i: (0, 0, i, 0)),
        row(DIFF_KV_WIDTH), row(DIFF_KV_WIDTH), row(DIFF_KV_WIDTH), tmajor(DIFF_KV_WIDTH), row(DIFF_WIDTH),
    )
    return pl.pallas_call(
        _even_in_kernel,
        out_shape=out_shape,
        grid=(n // tm,),
        in_specs=[row(d), _full(ng.shape), _full(win.shape), _full(qg.shape), _full(wuq.shape),
                  _full(wukt.shape), _full(kvg.shape), row(LANES), row(LANES)],
        out_specs=out_specs,
        compiler_params=_params(1),
        name="even_in",
    )(x, ng, win, qg, wuq, wukt, kvg, cos2, sin2)


def _flash_init(m_ref, l_ref, acc_ref):
    m_ref[...] = jnp.full(m_ref.shape, -jnp.inf, F32)
    l_ref[...] = jnp.zeros(l_ref.shape, F32)
    acc_ref[...] = jnp.zeros(acc_ref.shape, F32)


def _flash_update(s, v, m_ref, l_ref, acc_ref):
    m_prev = m_ref[...]
    m_new = jnp.maximum(m_prev, jnp.max(s, axis=-1, keepdims=True))
    alpha = jnp.exp2(m_prev - m_new)
    p = jnp.exp2(s - m_new)
    l_ref[...] = alpha * l_ref[...] + jnp.sum(p, axis=-1, keepdims=True)
    acc_ref[...] = alpha * acc_ref[...] + _dot(p.astype(BF16), v)
    m_ref[...] = m_new


def _flash_update_t(st, vt, m_ref, l_ref, acc_ref):
    m_prev = m_ref[...]
    m_new = jnp.maximum(m_prev, jnp.max(st, axis=0, keepdims=True))
    alpha = jnp.exp2(m_prev - m_new)
    p = jnp.exp2(st - m_new)
    l_ref[...] = alpha * l_ref[...] + jnp.sum(p, axis=0, keepdims=True)
    acc_ref[...] = alpha * acc_ref[...] + _dot(vt, p.astype(BF16))
    m_ref[...] = m_new


def _vt_tiles(vt_ref, j, w, rows):
    tiles = [vt_ref[j + u, rows, :] for u in range(w)]
    return tiles[0] if w == 1 else jnp.concatenate(tiles, axis=1)


def _far_tiles(n_far, logits, update, st0_ref, st1_ref):
    npairs = n_far // 2
    last = jnp.maximum(npairs - 1, 0)

    @pl.when(npairs > 0)
    def _():
        st0_ref[...] = logits(0, 2)

    def body(u, carry):
        a = 2 * u
        st1_ref[...] = logits(2 * jnp.minimum(a + 1, last), 2)
        update(st0_ref[...], 2 * a, 2)

        @pl.when(a + 1 < npairs)
        def _():
            st0_ref[...] = logits(2 * jnp.minimum(a + 2, last), 2)
            update(st1_ref[...], 2 * (a + 1), 2)

        return carry

    lax.fori_loop(0, (npairs + 1) // 2, body, 0)

    @pl.when(n_far % 2 == 1)
    def _():
        update(logits(n_far - 1, 1), n_far - 1, 1)


def _mla_prompt_kernel(q_ref, k_ref, vt_ref, mask_ref, o_ref, m_ref, l_ref, acc_ref, st0_ref, st1_ref):
    i = pl.program_id(0)
    t = ATT_TILE
    q = q_ref[...].reshape(MLA_HEADS * t, 2 * LANES)
    _flash_init(m_ref, l_ref, acc_ref)

    def logits(j, w):
        return _dot_nt(k_ref[pl.ds(pl.multiple_of(j * t, t), w * t), :], q)

    def update(st, j, w):
        _flash_update_t(st, _vt_tiles(vt_ref, j, w, slice(None)), m_ref, l_ref, acc_ref)

    _far_tiles(i, logits, update, st0_ref, st1_ref)
    update(logits(i, 1) + mask_ref[...], i, 1)
    o = (acc_ref[...] / l_ref[...]).T
    o_ref[...] = o.reshape(MLA_HEADS, t, MLA_KV_LORA)


def _mla_prompt(qm, kcat, ckvt, mask):
    s = kcat.shape[0]
    t = ATT_TILE
    rows = MLA_HEADS * t
    return pl.pallas_call(
        _mla_prompt_kernel,
        out_shape=jax.ShapeDtypeStruct((MLA_HEADS, s, MLA_KV_LORA), F32),
        grid=(s // t,),
        in_specs=[pl.BlockSpec((MLA_HEADS, t, 2 * LANES), lambda i: (0, i, 0)),
                  _resident(kcat.shape), _resident(ckvt.shape), _resident(mask.shape)],
        out_specs=pl.BlockSpec((MLA_HEADS, t, MLA_KV_LORA), lambda i: (0, i, 0)),
        scratch_shapes=[pltpu.VMEM((1, rows), F32), pltpu.VMEM((1, rows), F32),
                        pltpu.VMEM((MLA_KV_LORA, rows), F32),
                        pltpu.VMEM((2 * t, rows), F32), pltpu.VMEM((2 * t, rows), F32)],
        compiler_params=_params(1),
        name="mla_prompt",
    )(qm, kcat, ckvt, mask)


def _diff_lambda(lp, lambda_init):
    a = jnp.sum(lp[0:1] * lp[1:2], axis=-1, keepdims=True)
    b = jnp.sum(lp[2:3] * lp[3:4], axis=-1, keepdims=True)
    return jnp.exp(a) - jnp.exp(b) + lambda_init


def _diff_prompt_kernel(q_ref, k_ref, vt_ref, bias_ref, lp_ref, o_ref, m_ref, l_ref, acc_ref, st0_ref, st1_ref, *,
                        lambda_init):
    i = pl.program_id(0)
    t = ATT_TILE
    group = DIFF_HEADS // DIFF_KV_HEADS
    rows = 2 * group * t
    lam = _diff_lambda(lp_ref[...], lambda_init)
    for kv in range(DIFF_KV_HEADS):
        q = q_ref[kv].reshape(rows, LANES)
        cols = slice(kv * LANES, (kv + 1) * LANES)
        _flash_init(m_ref, l_ref, acc_ref)

        def logits(j, w, q=q, cols=cols):
            return _dot_nt(k_ref[pl.ds(pl.multiple_of(j * t, t), w * t), cols], q)

        def update(st, j, w, cols=cols):
            _flash_update_t(st, _vt_tiles(vt_ref, j, w, cols), m_ref, l_ref, acc_ref)

        _far_tiles(jnp.maximum(i - 1, 0), logits, update, st0_ref, st1_ref)

        @pl.when(i >= 1)
        def _(logits=logits, update=update, kv=kv):
            update(logits(i - 1, 1) + bias_ref[kv, 1], i - 1, 1)

        update(logits(i, 1) + bias_ref[kv, 0], i, 1)
        ot = acc_ref[...] / l_ref[...]
        o = (ot[:, :group * t] - lam * ot[:, group * t:]).T
        for g in range(group):
            hd = kv * group + g
            o_ref[:, hd * LANES:(hd + 1) * LANES] = o[g * t:(g + 1) * t]


def _diff_prompt(dq, dkb, dvt, bias, lp, lambda_init):
    s = dkb.shape[0]
    t = ATT_TILE
    rows = 4 * t
    return pl.pallas_call(
        functools.partial(_diff_prompt_kernel, lambda_init=lambda_init),
        out_shape=jax.ShapeDtypeStruct((s, DIFF_WIDTH), F32),
        grid=(s // t,),
        in_specs=[pl.BlockSpec((DIFF_KV_HEADS, 4, t, LANES), lambda i: (0, 0, i, 0)),
                  _resident(dkb.shape), _resident(dvt.shape), _resident(bias.shape), _full(lp.shape)],
        out_specs=pl.BlockSpec((t, DIFF_WIDTH), lambda i: (i, 0)),
        scratch_shapes=[pltpu.VMEM((1, rows), F32), pltpu.VMEM((1, rows), F32),
                        pltpu.VMEM((LANES, rows), F32),
                        pltpu.VMEM((2 * t, rows), F32), pltpu.VMEM((2 * t, rows), F32)],
        compiler_params=_params(1),
        name="diff_prompt",
    )(dq, dkb, dvt, bias, lp)


_O_Q = 0
_O_K = _O_Q + MOBA_WIDTH
_O_V = _O_K + MOBA_KV_WIDTH
_O_G = _O_V + MOBA_KV_WIDTH
_O_END = _O_G + MOBA_WIDTH


def _mid_kernel(x_ref, olat_ref, wuv_ref, diffo_ref, gm_ref, gd_ref, subg_ref, wout_ref, ng_ref, win_ref,
                x1_ref, q_ref, k_ref, kb_ref, v_ref, vt_ref, g_ref, mean_ref, *, sub_scale):
    mla = jnp.concatenate([_dot(olat_ref[hd].astype(BF16), wuv_ref[hd]) for hd in range(MLA_HEADS)], axis=-1)
    dn = jnp.concatenate(
        [_rms(diffo_ref[:, hd * LANES:(hd + 1) * LANES], subg_ref[...], SUBLN_EPS) * sub_scale
         for hd in range(DIFF_HEADS)], axis=-1)
    mix = jnp.concatenate([mla * _silu(gm_ref[...]), dn * _silu(gd_ref[...])], axis=-1).astype(BF16)
    x1 = x_ref[...] + _dot(mix, wout_ref[...])
    x1_ref[...] = x1
    h = _rms(x1, ng_ref[...], NORM_EPS).astype(BF16)
    z = _dot(h, win_ref[...])
    q_ref[...] = z[:, _O_Q:_O_K]
    k = z[:, _O_K:_O_V]
    v = z[:, _O_V:_O_G]
    k_ref[...] = k
    kb_ref[...] = k.astype(BF16)
    v_ref[...] = v
    vt_ref[0] = v.T.astype(BF16)
    g_ref[...] = z[:, _O_G:_O_END]
    mean_ref[0] = jnp.mean(k, axis=0, keepdims=True)


def _mid(x, olat, wuv, diffo, gm, gd, subg, wout, ng, win, sub_scale):
    n, d = x.shape
    tm = min(ROW_TILE, n)
    row = lambda w: pl.BlockSpec((tm, w), lambda i: (i, 0))
    out_shape = (
        jax.ShapeDtypeStruct((n, d), F32), jax.ShapeDtypeStruct((n, MOBA_WIDTH), F32),
        jax.ShapeDtypeStruct((n, MOBA_KV_WIDTH), F32), jax.ShapeDtypeStruct((n, MOBA_KV_WIDTH), BF16),
        jax.ShapeDtypeStruct((n, MOBA_KV_WIDTH), F32), jax.ShapeDtypeStruct((n // tm, MOBA_KV_WIDTH, tm), BF16),
        jax.ShapeDtypeStruct((n, MOBA_WIDTH), F32), jax.ShapeDtypeStruct((n // tm, 1, MOBA_KV_WIDTH), F32),
    )
    out_specs = (row(d), row(MOBA_WIDTH), row(MOBA_KV_WIDTH), row(MOBA_KV_WIDTH), row(MOBA_KV_WIDTH),
                 pl.BlockSpec((1, MOBA_KV_WIDTH, tm), lambda i: (i, 0, 0)), row(MOBA_WIDTH),
                 pl.BlockSpec((1, 1, MOBA_KV_WIDTH), lambda i: (i, 0, 0)))
    return pl.pallas_call(
        functools.partial(_mid_kernel, sub_scale=sub_scale),
        out_shape=out_shape,
        grid=(n // tm,),
        in_specs=[row(d), pl.BlockSpec((MLA_HEADS, tm, MLA_KV_LORA), lambda i: (0, i, 0)), _full(wuv.shape),
                  row(DIFF_WIDTH), row(MLA_WIDTH), row(DIFF_WIDTH), _full(subg.shape), _full(wout.shape),
                  _full(ng.shape), _full(win.shape)],
        out_specs=out_specs,
        compiler_params=_params(1),
        name="even_out_odd_in",
    )(x, olat, wuv, diffo, gm, gd, subg, wout, ng, win)


def _topk_mask(gate, ksel):
    lane = lax.broadcasted_iota(jnp.int32, gate.shape, 1).astype(F32)
    sel = jnp.zeros(gate.shape, jnp.bool_)
    for _ in range(ksel):
        mx = jnp.max(gate, axis=-1, keepdims=True)
        idx = jnp.min(jnp.where(gate == mx, lane, float(gate.shape[1])), axis=-1, keepdims=True)
        pick = lane == idx
        sel = jnp.logical_or(sel, pick)
        gate = jnp.where(pick, -jnp.inf, gate)
    return sel


def _moba_prompt_kernel(q_ref, mean_ref, k_ref, vt_ref, bias_ref, o_ref, m_ref, l_ref, acc_ref, st0_ref, st1_ref, *,
                        nblocks):
    i = pl.program_id(0)
    t = ATT_TILE
    nbp = mean_ref.shape[0]
    group = MOBA_HEADS // MOBA_KV_HEADS
    rows = group * t
    scale = MOBA_HEAD_DIM ** -0.5 * LOG2E
    blk = lax.broadcasted_iota(jnp.int32, (rows, nbp), 1)
    for kv in range(MOBA_KV_HEADS):
        cols = slice(kv * LANES, (kv + 1) * LANES)
        qf = jnp.concatenate(
            [q_ref[:, (kv * group + g) * LANES:(kv * group + g + 1) * LANES] for g in range(group)], axis=0)
        gate = lax.dot_general(qf, mean_ref[:, cols], (((1,), (1,)), ((), ())),
                               precision=lax.Precision.HIGHEST, preferred_element_type=F32)
        past = blk < i
        sel = jnp.logical_and(_topk_mask(jnp.where(past, gate, NEG_INF), min(MOBA_TOPK, nblocks)), past)
        q = jnp.concatenate([qf * scale, jnp.where(sel, 0.0, NEG_INF)], axis=-1).astype(BF16)
        _flash_init(m_ref, l_ref, acc_ref)

        def logits(j, w, masked=True, q=q, cols=cols):
            k = k_ref[pl.ds(pl.multiple_of(j * t, t), w * t), cols]
            kblk = j + lax.broadcasted_iota(jnp.int32, (w * t, nbp), 0) // t
            hot = jnp.logical_and(lax.broadcasted_iota(jnp.int32, (w * t, nbp), 1) == kblk, masked)
            return _dot_nt(jnp.concatenate([k, jnp.where(hot, 1.0, 0.0).astype(BF16)], axis=-1), q)

        def update(st, j, w, cols=cols):
            _flash_update_t(st, _vt_tiles(vt_ref, j, w, cols), m_ref, l_ref, acc_ref)

        _far_tiles(jnp.maximum(i - 1, 0), logits, update, st0_ref, st1_ref)

        @pl.when(i >= 1)
        def _(logits=logits, update=update, kv=kv):
            update(logits(i - 1, 1) + bias_ref[kv, 1], i - 1, 1)

        update(logits(i, 1, masked=False) + bias_ref[kv, 0], i, 1)
        o = (acc_ref[...] / l_ref[...]).T
        for g in range(group):
            hd = kv * group + g
            o_ref[:, hd * LANES:(hd + 1) * LANES] = o[g * t:(g + 1) * t]


def _moba_prompt(q, means, kb, vt, bias, nblocks):
    s = kb.shape[0]
    t = ATT_TILE
    rows = (MOBA_HEADS // MOBA_KV_HEADS) * t
    return pl.pallas_call(
        functools.partial(_moba_prompt_kernel, nblocks=nblocks),
        out_shape=jax.ShapeDtypeStruct((s, MOBA_WIDTH), F32),
        grid=(s // t,),
        in_specs=[pl.BlockSpec((t, MOBA_WIDTH), lambda i: (i, 0)), _resident(means.shape),
                  _resident(kb.shape), _resident(vt.shape), _resident(bias.shape)],
        out_specs=pl.BlockSpec((t, MOBA_WIDTH), lambda i: (i, 0)),
        scratch_shapes=[pltpu.VMEM((1, rows), F32), pltpu.VMEM((1, rows), F32),
                        pltpu.VMEM((LANES, rows), F32),
                        pltpu.VMEM((2 * t, rows), F32), pltpu.VMEM((2 * t, rows), F32)],
        compiler_params=_params(1),
        name="moba_prompt",
    )(q, means, kb, vt, bias)


def _final_kernel(x_ref, o_ref, g_ref, wout_ref, fg_ref, y_ref):
    mix = (o_ref[...] * _silu(g_ref[...])).astype(BF16)
    x2 = x_ref[...] + _dot(mix, wout_ref[...])
    y_ref[...] = _rms(x2, fg_ref[...], NORM_EPS)


def _final(x1, o, g, wout, fg):
    n, d = x1.shape
    tm = min(ROW_TILE, n)
    row = lambda w: pl.BlockSpec((tm, w), lambda i: (i, 0))
    return pl.pallas_call(
        _final_kernel,
        out_shape=jax.ShapeDtypeStruct((n, d), F32),
        grid=(n // tm,),
        in_specs=[row(d), row(MOBA_WIDTH), row(MOBA_WIDTH), _full(wout.shape), _full(fg.shape)],
        out_specs=row(d),
        compiler_params=_params(1),
        name="odd_out_final",
    )(x1, o, g, wout, fg)


PAGE_RING = 3


def _fetch_pages(pt_ref, caches, bufs, sems, n):
    b, c = pl.program_id(0), pl.program_id(1)
    steps = pl.num_programs(1)
    total = pl.num_programs(0) * steps
    step = b * steps + c
    ahead = PAGE_RING - 1

    def copies(flat, lookup):
        sslot = lax.rem(flat, PAGE_RING)
        sb, sc = lax.div(flat, steps), lax.rem(flat, steps)
        for p in range(n):
            page = pt_ref[sb, sc * n + p] if lookup else 0
            for cache, buf, sem in zip(caches, bufs, sems):
                yield pltpu.make_async_copy(cache.at[page], buf.at[sslot, p], sem.at[sslot])

    def start(flat):
        @pl.when(flat < total)
        def _():
            for cp in copies(flat, True):
                cp.start()

    @pl.when(step == 0)
    def _():
        for k in range(ahead):
            start(step + k)

    start(step + ahead)
    for cp in copies(step, False):
        cp.wait()
    return lax.rem(step, PAGE_RING)


def _page_scratch(caches, n):
    bufs = [pltpu.VMEM((PAGE_RING, n) + c.shape[1:], c.dtype) for c in caches]
    return bufs + [pltpu.SemaphoreType.DMA((PAGE_RING,)) for _ in caches]


def _hbm_specs(caches):
    return [pl.BlockSpec(memory_space=pl.ANY) for _ in caches]


def _seq_spec(shape):
    return pl.BlockSpec((1,) + shape, lambda b, c, pt: (b,) + (0,) * len(shape))


def _const_spec(a):
    return pl.BlockSpec(a.shape, lambda b, c, pt: (0,) * a.ndim)


def _mla_sample_kernel(pt_ref, q_ref, knew_ref, mask_ref, ckv_hbm, krt_hbm, o_ref, m_ref, l_ref, acc_ref,
                       ckv_buf, krt_buf, ckv_sem, krt_sem, *, n):
    slot = _fetch_pages(pt_ref, (ckv_hbm, krt_hbm), (ckv_buf, krt_buf), (ckv_sem, krt_sem), n)
    c = pl.program_id(1)
    q = q_ref[0]

    @pl.when(c == 0)
    def _():
        _flash_init(m_ref, l_ref, acc_ref)

    ckv = ckv_buf[slot].reshape(n * PAGE_SIZE, MLA_KV_LORA).astype(BF16)
    krt = jnp.concatenate([krt_buf[slot, p].astype(BF16) for p in range(n)], axis=1)
    s = _dot_nt(q[:, :MLA_KV_LORA], ckv) + _dot(q[:, MLA_KV_LORA:MLA_KV_LORA + MLA_ROPE], krt)
    _flash_update(s, ckv, m_ref, l_ref, acc_ref)

    @pl.when(c == pl.num_programs(1) - 1)
    def _():
        knew = knew_ref[0]
        _flash_update(_dot_nt(q, knew) + mask_ref[...], knew[:, :MLA_KV_LORA], m_ref, l_ref, acc_ref)
        o_ref[0] = acc_ref[...] / l_ref[...]


def _mla_sample(page_table, q, knew, mask, cache_ckv, cache_krt):
    nseq, rows, _ = q.shape
    n = min(MLA_PAGES_PER_STEP, page_table.shape[1])
    assert page_table.shape[1] % n == 0
    steps = page_table.shape[1] // n
    caches = (cache_ckv, cache_krt)
    grid_spec = pltpu.PrefetchScalarGridSpec(
        num_scalar_prefetch=1,
        grid=(nseq, steps),
        in_specs=[_seq_spec((rows, 2 * LANES)), _seq_spec((LANES, 2 * LANES)), _const_spec(mask)]
        + _hbm_specs(caches),
        out_specs=_seq_spec((rows, MLA_KV_LORA)),
        scratch_shapes=[pltpu.VMEM((rows, 1), F32), pltpu.VMEM((rows, 1), F32),
                        pltpu.VMEM((rows, MLA_KV_LORA), F32)] + _page_scratch(caches, n),
    )
    return pl.pallas_call(
        functools.partial(_mla_sample_kernel, n=n),
        out_shape=jax.ShapeDtypeStruct((nseq, rows, MLA_KV_LORA), F32),
        grid_spec=grid_spec,
        compiler_params=_params(2),
        name="mla_sample",
    )(page_table, q, knew, mask, *caches)


def _diff_sample_kernel(pt_ref, q_ref, knew_ref, vnew_ref, pmask_ref, blast_ref, bnew_ref, lp_ref, k_hbm, v_hbm,
                        o_ref, m_ref, l_ref, acc_ref, k_buf, v_buf, k_sem, v_sem, *, n, lambda_init):
    slot = _fetch_pages(pt_ref, (k_hbm, v_hbm), (k_buf, v_buf), (k_sem, v_sem), n)
    c = pl.program_id(1)
    last = pl.num_programs(1) - 1
    q = q_ref[0]
    step_rows = n * k_buf.shape[2]

    @pl.when(c == 0)
    def _():
        _flash_init(m_ref, l_ref, acc_ref)

    def pages(bias_ref):
        k = k_buf[slot].reshape(step_rows, LANES).astype(BF16)
        v = v_buf[slot].reshape(step_rows, LANES).astype(BF16)
        _flash_update(_dot_nt(q, k) + bias_ref[...], v, m_ref, l_ref, acc_ref)

    @pl.when(c != last)
    def _():
        pages(pmask_ref)

    @pl.when(c == last)
    def _():
        pages(blast_ref)
        _flash_update(_dot_nt(q, knew_ref[0].astype(BF16)) + bnew_ref[...], vnew_ref[0].astype(BF16),
                      m_ref, l_ref, acc_ref)
        lam = _diff_lambda(lp_ref[...], lambda_init)
        o = acc_ref[...] / l_ref[...]
        per_kv = q.shape[0] // DIFF_KV_HEADS
        half = per_kv // 2
        for kv in range(DIFF_KV_HEADS):
            r0 = kv * per_kv
            o_ref[0, kv] = o[r0:r0 + half] - lam * o[r0 + half:r0 + per_kv]


def _diff_sample(page_table, q, knew, vnew, pmask, blast, bnew, lp, cache_k, cache_v, lambda_init):
    nseq, rows, _ = q.shape
    n = min(PAGES_PER_STEP, page_table.shape[1])
    steps = page_table.shape[1] // n
    prow = cache_k.shape[1]
    caches = (cache_k, cache_v)
    grid_spec = pltpu.PrefetchScalarGridSpec(
        num_scalar_prefetch=1,
        grid=(nseq, steps),
        in_specs=[_seq_spec((rows, LANES)), _seq_spec((prow, LANES)), _seq_spec((prow, LANES)),
                  _const_spec(pmask), _const_spec(blast), _const_spec(bnew), _const_spec(lp)]
        + _hbm_specs(caches),
        out_specs=_seq_spec((DIFF_KV_HEADS, rows // 4, LANES)),
        scratch_shapes=[pltpu.VMEM((rows, 1), F32), pltpu.VMEM((rows, 1), F32), pltpu.VMEM((rows, LANES), F32)]
        + _page_scratch(caches, n),
    )
    return pl.pallas_call(
        functools.partial(_diff_sample_kernel, n=n, lambda_init=lambda_init),
        out_shape=jax.ShapeDtypeStruct((nseq, DIFF_KV_HEADS, rows // 4, LANES), F32),
        grid_spec=grid_spec,
        compiler_params=_params(2),
        name="diff_sample",
    )(page_table, q, knew, vnew, pmask, blast, bnew, lp, *caches)


def _moba_sample_kernel(pt_ref, q_ref, knew_ref, vnew_ref, pmask_ref, blast_ref, bnew_ref, k_hbm, v_hbm,
                        o_ref, gate_ref, mb_ref, lb_ref, ob_ref, k_buf, v_buf, k_sem, v_sem, *, n, nblocks):
    slot = _fetch_pages(pt_ref, (k_hbm, v_hbm), (k_buf, v_buf), (k_sem, v_sem), n)
    c = pl.program_id(1)
    last = pl.num_programs(1) - 1
    ppb = MOBA_BLOCK // PAGE_SIZE
    bps = n // ppb
    prow = k_buf.shape[2]
    brow = ppb * prow
    rows = q_ref.shape[1]
    qf = q_ref[0]
    qb = (qf * (MOBA_HEAD_DIM ** -0.5 * LOG2E)).astype(BF16)
    lane = lax.broadcasted_iota(jnp.int32, (rows, nblocks), 1)
    row_kv = lax.broadcasted_iota(jnp.int32, (rows, LANES), 0) // (rows // MOBA_KV_HEADS)
    key_kv = lax.broadcasted_iota(jnp.int32, (brow, LANES), 0) % MOBA_KV_HEADS

    @pl.when(c == 0)
    def _():
        for ref in (gate_ref, mb_ref, lb_ref):
            ref[...] = jnp.zeros(ref.shape, F32)

    def put(ref, blk, col):
        ref[...] = jnp.where(lane == blk, col, ref[...])

    def blocks(last_bias_ref):
        kf = k_buf[slot].reshape(n * prow, LANES)
        vb = v_buf[slot].reshape(n * prow, LANES).astype(BF16)
        s_all = _dot_nt(qb, kf.astype(BF16))
        for bi in range(bps):
            blk = c * bps + bi
            kblk = kf[bi * brow:(bi + 1) * brow]
            kmean = jnp.zeros((rows, LANES), F32)
            for kv in range(MOBA_KV_HEADS):
                mean_kv = jnp.sum(jnp.where(key_kv == kv, kblk, 0.0), axis=0, keepdims=True) * (1.0 / MOBA_BLOCK)
                kmean = jnp.where(row_kv == kv, mean_kv, kmean)
            put(gate_ref, blk, jnp.sum(qf * kmean, axis=-1, keepdims=True))
            bias_ref = last_bias_ref if bi == bps - 1 else pmask_ref
            s = s_all[:, bi * brow:(bi + 1) * brow] + bias_ref[...]
            m = jnp.max(s, axis=-1, keepdims=True)
            p = jnp.exp2(s - m)
            put(mb_ref, blk, m)
            put(lb_ref, blk, jnp.sum(p, axis=-1, keepdims=True))
            ob_ref[blk] = _dot(p.astype(BF16), vb[bi * brow:(bi + 1) * brow])

    @pl.when(c != last)
    def _():
        blocks(pmask_ref)

    @pl.when(c == last)
    def _():
        blocks(blast_ref)
        s = _dot_nt(qb, knew_ref[0].astype(BF16)) + bnew_ref[...]
        m_new = jnp.max(s, axis=-1, keepdims=True)
        p = jnp.exp2(s - m_new)
        l_new = jnp.sum(p, axis=-1, keepdims=True)
        o_new = _dot(p.astype(BF16), vnew_ref[0].astype(BF16))
        sel = _topk_mask(gate_ref[...], min(MOBA_TOPK, nblocks))
        mb = mb_ref[...]
        m_all = jnp.maximum(jnp.max(jnp.where(sel, mb, -jnp.inf), axis=-1, keepdims=True), m_new)
        w = jnp.where(sel, jnp.exp2(mb - m_all), 0.0)
        w_new = jnp.exp2(m_new - m_all)
        den = jnp.sum(w * lb_ref[...], axis=-1, keepdims=True) + w_new * l_new
        num = w_new * o_new
        for blk in range(nblocks):
            num = num + w[:, blk:blk + 1] * ob_ref[blk]
        o_ref[0] = num / den


def _moba_sample(page_table, q, knew, vnew, pmask, blast, bnew, cache_k, cache_v):
    nseq, rows, _ = q.shape
    ppb = MOBA_BLOCK // PAGE_SIZE
    nblocks = page_table.shape[1] // ppb
    n = min(PAGES_PER_STEP, page_table.shape[1])
    steps = page_table.shape[1] // n
    prow = cache_k.shape[1]
    caches = (cache_k, cache_v)
    grid_spec = pltpu.PrefetchScalarGridSpec(
        num_scalar_prefetch=1,
        grid=(nseq, steps),
        in_specs=[_seq_spec((rows, LANES)), _seq_spec((prow, LANES)), _seq_spec((prow, LANES)),
                  _const_spec(pmask), _const_spec(blast), _const_spec(bnew)]
        + _hbm_specs(caches),
        out_specs=_seq_spec((rows, LANES)),
        scratch_shapes=[pltpu.VMEM((rows, nblocks), F32), pltpu.VMEM((rows, nblocks), F32),
                        pltpu.VMEM((rows, nblocks), F32), pltpu.VMEM((nblocks, rows, LANES), F32)]
        + _page_scratch(caches, n),
    )
    return pl.pallas_call(
        functools.partial(_moba_sample_kernel, n=n, nblocks=nblocks),
        out_shape=jax.ShapeDtypeStruct((nseq, rows, LANES), F32),
        grid_spec=grid_spec,
        compiler_params=_params(2),
        name="moba_sample",
    )(page_table, q, knew, vnew, pmask, blast, bnew, *caches)


def _rope_tables(pos):
    half = MLA_ROPE // 2
    inv = ROPE_BASE ** (-jnp.arange(half, dtype=F32) / half)
    ang = pos.astype(F32)[:, None] * inv
    pad = jnp.zeros((pos.shape[0], LANES - 2 * half), F32)
    cos = jnp.cos(ang)
    sin = jnp.sin(ang)
    return jnp.concatenate([cos, cos, pad], axis=-1), jnp.concatenate([sin, sin, pad], axis=-1)


def _lane_pad(w, width):
    return jnp.pad(w, ((0, 0), (0, width - w.shape[1])))


def _even_weights(w_in, w_uq, w_uk, w_uv):
    half = MLA_ROPE // 2
    sizes = [MLA_Q_LORA, MLA_KV_LORA, MLA_ROPE, MLA_WIDTH, DIFF_WIDTH, DIFF_KV_WIDTH, DIFF_KV_WIDTH]
    cq, ckv, kr, gm, dq, dk, dv, gd = jnp.split(w_in, np.cumsum(sizes), axis=1)
    ka = _lane_pad(kr, LANES)
    kb = _lane_pad(jnp.concatenate([-kr[:, half:], kr[:, :half]], axis=1), LANES)
    win = jnp.concatenate([gm, dq, dk, dv, gd, ckv, _lane_pad(cq, 2 * LANES), ka, kb], axis=1).astype(BF16)
    nope = jnp.concatenate([_lane_pad(w_uq[:, h, :MLA_NOPE], LANES) for h in range(MLA_HEADS)], axis=1)
    ra = jnp.concatenate([_lane_pad(w_uq[:, h, MLA_NOPE:], LANES) for h in range(MLA_HEADS)], axis=1)
    rb = jnp.concatenate(
        [_lane_pad(jnp.concatenate([-w_uq[:, h, MLA_NOPE + half:], w_uq[:, h, MLA_NOPE:MLA_NOPE + half]], axis=1),
                   LANES) for h in range(MLA_HEADS)], axis=1)
    wuq = jnp.concatenate([nope, ra, rb], axis=1).astype(BF16)
    wukt = jnp.pad(jnp.transpose(w_uk, (1, 2, 0)), ((0, 0), (0, LANES - MLA_NOPE), (0, 0))).astype(BF16)
    wuv = jnp.transpose(w_uv, (1, 0, 2)).astype(BF16)
    return win, wuq, wukt, wuv


def _interleave_kv(tile):
    nkv, r, j = tile.shape
    own = jnp.eye(nkv, dtype=jnp.bool_)[:, None, None, :]
    return jnp.where(own, tile[..., None], NEG_INF).reshape(nkv * r, j * nkv)


def kernel(x_prompt, x_sample, cache_mla_ckv, cache_mla_krope, cache_diff_k, cache_diff_v, cache_moba_k, cache_moba_v, page_table, norm_g, final_norm_g, rel_bias, w_in_even, mla_q_norm_g, mla_w_uq, mla_kv_norm_g, mla_w_uk, mla_w_uv, diff_lambda, diff_subln_g, w_out_even, w_in_odd, w_out_odd):
    _, s_len, d_model = x_prompt.shape
    n_dec, t_len, _ = x_sample.shape
    n_pages = page_table.shape[1]
    past_len = n_pages * PAGE_SIZE
    n_smp = n_dec * t_len
    t = ATT_TILE
    n_step = min(PAGES_PER_STEP, n_pages)
    assert norm_g.shape[0] == 2 and x_prompt.shape[0] == 1
    assert ROW_TILE == MOBA_BLOCK == ATT_TILE
    assert s_len % t == 0 and n_smp % min(ROW_TILE, n_smp) == 0 and past_len % MOBA_BLOCK == 0
    assert n_pages % n_step == 0 and n_step % (MOBA_BLOCK // PAGE_SIZE) == 0 and t_len <= 8
    assert FAR_DIST <= PAGE_SIZE + 1 and FAR_DIST <= t + 1
    dgroup = DIFF_HEADS // DIFF_KV_HEADS
    mgroup = MOBA_HEADS // MOBA_KV_HEADS
    nblocks_p = s_len // MOBA_BLOCK
    lambda_init = 0.8 - 0.6 * math.exp(-0.3 * 0)

    tab = jnp.pad(rel_bias.astype(F32), ((0, 0), (0, 8)))
    zero_col = rel_bias.shape[1]
    band0 = _toeplitz_bias(tab, t, t, 0, True, transposed=True)
    band1 = _toeplitz_bias(tab, t, t, t, False, transposed=True)
    s_last_page = _toeplitz_bias(tab, 8, PAGE_SIZE, PAGE_SIZE, False)
    s_last_block = _toeplitz_bias(tab, 8, MOBA_BLOCK, MOBA_BLOCK, False)
    s_new = _toeplitz_bias(tab, 8, LANES, 0, True)
    diff_cols = [[m * DIFF_HEADS + kv * dgroup + g for m in range(2) for g in range(dgroup)]
                 for kv in range(DIFF_KV_HEADS)]
    moba_cols = [[kv * mgroup + g for g in range(mgroup)] for kv in range(MOBA_KV_HEADS)]
    lanes_of = lambda tb, cols: jnp.stack([jnp.concatenate([tb[c] for c in cl], axis=1) for cl in cols])
    rows_of = lambda tb, cols: jnp.stack([jnp.concatenate([tb[c, :t_len] for c in cl], axis=0) for cl in cols])
    diff_band = jnp.stack([lanes_of(band0, diff_cols), lanes_of(band1, diff_cols)], axis=1)
    moba_band = jnp.stack([lanes_of(band0, moba_cols), lanes_of(band1, moba_cols)], axis=1)
    mla_mask = jnp.tile(band0[zero_col], (1, MLA_HEADS))
    mla_new_mask = jnp.tile(s_new[zero_col, :t_len], (MLA_HEADS, 1))
    drows = 2 * dgroup * t_len
    diff_pmask = _interleave_kv(jnp.zeros((DIFF_KV_HEADS, drows, n_step * PAGE_SIZE), F32))
    diff_last = jnp.concatenate([diff_pmask[:, :(n_step - 1) * 2 * PAGE_SIZE],
                                 _interleave_kv(rows_of(s_last_page, diff_cols))], axis=1)
    diff_new = _interleave_kv(rows_of(s_new, diff_cols))
    moba_pmask = _interleave_kv(jnp.zeros((MOBA_KV_HEADS, mgroup * t_len, MOBA_BLOCK), F32))
    moba_last = _interleave_kv(rows_of(s_last_block, moba_cols))
    moba_new = _interleave_kv(rows_of(s_new, moba_cols))

    win_e, wuq, wukt, wuv = _even_weights(w_in_even[0], mla_w_uq[0], mla_w_uk[0], mla_w_uv[0])
    wout_e = w_out_even[0].astype(BF16)
    win_o = w_in_odd[0].astype(BF16)
    wout_o = w_out_odd[0].astype(BF16)
    ng0, ng1, fg = norm_g[0][None], norm_g[1][None], final_norm_g[None]
    qg, kvg, subg = mla_q_norm_g[0][None], mla_kv_norm_g[0][None], diff_subln_g[0][None]
    lp = diff_lambda[0].astype(F32)

    xp = x_prompt[0]
    xs = x_sample.reshape(n_smp, d_model)
    cos_p, sin_p = _rope_tables(jnp.arange(s_len))
    cos_s, sin_s = _rope_tables(jnp.tile(past_len + jnp.arange(t_len), n_dec))

    krt_cache = jnp.swapaxes(cache_mla_krope, 2, 3)
    kv_rows = lambda a: a[0].reshape(a.shape[1], a.shape[2] * a.shape[3], a.shape[4])
    new_rows = lambda a, nkv: jnp.pad(a.reshape(n_dec, t_len * nkv, LANES),
                                      ((0, 0), (0, (PAGE_SIZE - t_len) * nkv), (0, 0)))

    (qm_p, kcat_p, ckvt_p, ckv_p, kr_p, gm_p, dq_p, dk_p, dkb_p, dv_p, dvt_p, gd_p) = _even_in(
        xp, ng0, win_e, qg, wuq, wukt, kvg, cos_p, sin_p)
    olat_p = _mla_prompt(qm_p, kcat_p, ckvt_p, mla_mask)
    diffo_p = _diff_prompt(dq_p, dkb_p, dvt_p, diff_band, lp, lambda_init)
    (x1_p, q_p, k_p, kb_p, v_p, vt_p, g_p, means_p) = _mid(
        xp, olat_p, wuv, diffo_p, gm_p, gd_p, subg, wout_e, ng1, win_o, 1.0 - lambda_init)

    (qm_s, kcat_s, _, ckv_s, kr_s, gm_s, dq_s, dk_s, _, dv_s, _, gd_s) = _even_in(
        xs, ng0, win_e, qg, wuq, wukt, kvg, cos_s, sin_s)
    q_mla = qm_s.reshape(MLA_HEADS, n_dec, t_len, 2 * LANES).transpose(1, 0, 2, 3).reshape(
        n_dec, MLA_HEADS * t_len, 2 * LANES)
    knew_mla = jnp.pad(kcat_s.reshape(n_dec, t_len, 2 * LANES), ((0, 0), (0, LANES - t_len), (0, 0)))
    olat_s = _mla_sample(page_table, q_mla, knew_mla, mla_new_mask, cache_mla_ckv[0], krt_cache[0])
    olat_s = olat_s.reshape(n_dec, MLA_HEADS, t_len, MLA_KV_LORA).transpose(1, 0, 2, 3).reshape(
        MLA_HEADS, n_smp, MLA_KV_LORA)
    q_diff = dq_s.reshape(DIFF_KV_HEADS, 4, n_dec, t_len, LANES).transpose(2, 0, 1, 3, 4).reshape(
        n_dec, DIFF_KV_HEADS * drows, LANES)
    diffo_s = _diff_sample(page_table, q_diff, new_rows(dk_s, DIFF_KV_HEADS), new_rows(dv_s, DIFF_KV_HEADS),
                           diff_pmask, diff_last, diff_new, lp, kv_rows(cache_diff_k), kv_rows(cache_diff_v),
                           lambda_init)
    diffo_s = diffo_s.reshape(n_dec, DIFF_KV_HEADS, dgroup, t_len, LANES).transpose(0, 3, 1, 2, 4).reshape(
        n_smp, DIFF_WIDTH)
    (x1_s, q_s, k_s, _, v_s, _, g_s, _) = _mid(
        xs, olat_s, wuv, diffo_s, gm_s, gd_s, subg, wout_e, ng1, win_o, 1.0 - lambda_init)

    means = means_p.reshape(nblocks_p, MOBA_KV_WIDTH)
    means = jnp.pad(means, ((0, -nblocks_p % LANES), (0, 0)))
    o_p = _moba_prompt(q_p, means, kb_p, vt_p, moba_band, nblocks_p)
    y_p = _final(x1_p, o_p, g_p, wout_o, fg)

    q_moba = q_s.reshape(n_dec, t_len, MOBA_KV_HEADS, mgroup, LANES).transpose(0, 2, 3, 1, 4).reshape(
        n_dec, MOBA_HEADS * t_len, LANES)
    o_s = _moba_sample(page_table, q_moba, new_rows(k_s, MOBA_KV_HEADS), new_rows(v_s, MOBA_KV_HEADS),
                       moba_pmask, moba_last, moba_new, kv_rows(cache_moba_k), kv_rows(cache_moba_v))
    o_s = o_s.reshape(n_dec, MOBA_KV_HEADS, mgroup, t_len, LANES).transpose(0, 3, 1, 2, 4).reshape(
        n_smp, MOBA_WIDTH)
    y_s = _final(x1_s, o_s, g_s, wout_o, fg)

    kv4 = lambda a, b, s: a.reshape(1, b, s, 2, LANES)
    return (y_p[None], y_s.reshape(n_dec, t_len, d_model),
            ckv_p.reshape(1, 1, s_len, MLA_KV_LORA), kr_p.reshape(1, 1, s_len, MLA_ROPE),
            kv4(dk_p, 1, s_len), kv4(dv_p, 1, s_len), kv4(k_p, 1, s_len), kv4(v_p, 1, s_len),
            ckv_s.reshape(1, n_dec, t_len, MLA_KV_LORA), kr_s.reshape(1, n_dec, t_len, MLA_ROPE),
            kv4(dk_s, n_dec, t_len), kv4(dv_s, n_dec, t_len), kv4(k_s, n_dec, t_len), kv4(v_s, n_dec, t_len))
```

```python
import functools
import math

import jax
import jax.numpy as jnp
import numpy as np
from jax import lax
from jax.experimental import pallas as pl
from jax.experimental.pallas import tpu as pltpu

F32 = jnp.float32
BF16 = jnp.bfloat16

PAGE_SIZE = 128
MLA_HEADS = 8
MLA_Q_LORA = 192
MLA_KV_LORA = 128
MLA_NOPE = 64
MLA_ROPE = 32
MLA_V = 64
ROPE_BASE = 10000.0
DIFF_HEADS = 4
DIFF_KV_HEADS = 2
DIFF_HEAD_DIM = 64
SUBLN_EPS = 1e-5
MOBA_HEADS = 8
MOBA_KV_HEADS = 2
MOBA_HEAD_DIM = 128
MOBA_BLOCK = 256
MOBA_TOPK = 3
NUM_BUCKETS = 32
REL_MAX_DISTANCE = 128
REL_MAX_EXACT = NUM_BUCKETS // 2
NORM_EPS = 1e-6
NEG_INF = -1e30
LOG2E = math.log2(math.e)

LANES = 128
ROW_TILE = 256
ATT_TILE = 256
PAGES_PER_STEP = 16
MLA_PAGES_PER_STEP = 64
MLA_HEAD_PASSES = 2
VMEM_LIMIT = 56 * 1024 * 1024

MLA_WIDTH = MLA_HEADS * MLA_V
DIFF_WIDTH = DIFF_HEADS * 2 * DIFF_HEAD_DIM
DIFF_KV_WIDTH = DIFF_KV_HEADS * 2 * DIFF_HEAD_DIM
MOBA_WIDTH = MOBA_HEADS * MOBA_HEAD_DIM
MOBA_KV_WIDTH = MOBA_KV_HEADS * MOBA_HEAD_DIM


def _bucket_upper_bounds():
    d = np.arange(0, 4 * REL_MAX_DISTANCE)
    v = np.log(np.maximum(d, 1) / REL_MAX_EXACT) / math.log(REL_MAX_DISTANCE / REL_MAX_EXACT)
    v = v * (NUM_BUCKETS - REL_MAX_EXACT)
    frac = np.abs(v[REL_MAX_EXACT + 1:REL_MAX_DISTANCE] - np.round(v[REL_MAX_EXACT + 1:REL_MAX_DISTANCE]))
    assert frac.min() > 1e-3
    b = np.where(d < REL_MAX_EXACT, d, np.minimum(REL_MAX_EXACT + np.floor(v).astype(np.int64), NUM_BUCKETS - 1))
    hi = [int(d[b == k].max()) if np.any(b == k) else None for k in range(NUM_BUCKETS)]
    far = int(hi[NUM_BUCKETS - 2]) + 1
    return hi, far


BUCKET_HI, FAR_DIST = _bucket_upper_bounds()


def _dot(a, b):
    return jnp.dot(a, b, preferred_element_type=F32)


def _dot_nt(a, b):
    return lax.dot_general(a, b, (((1,), (1,)), ((), ())), preferred_element_type=F32)


def _rms(x, g, eps):
    return x * lax.rsqrt(jnp.mean(x * x, axis=-1, keepdims=True) + eps) * g


def _silu(g):
    return g / (1.0 + jnp.exp(-g))


def _full(shape):
    nd = len(shape)
    return pl.BlockSpec(shape, lambda *_: (0,) * nd)


def _resident(shape):
    nd = len(shape)
    return pl.BlockSpec(shape, lambda *_: (0,) * nd, pipeline_mode=pl.Buffered(1))


def _params(n_axes):
    return pltpu.CompilerParams(dimension_semantics=("arbitrary",) * n_axes, vmem_limit_bytes=VMEM_LIMIT)


def _toeplitz_kernel(tab_ref, out_ref, *, off, causal, transposed):
    c = pl.program_id(0)
    shape = out_ref.shape[1:]
    r = lax.broadcasted_iota(jnp.int32, shape, 0)
    s = lax.broadcasted_iota(jnp.int32, shape, 1)
    q, k = (s, r) if transposed else (r, s)
    dist = off + q - k
    acc = jnp.zeros(shape, F32) + tab_ref[NUM_BUCKETS - 1, c]
    for b in range(NUM_BUCKETS - 2, -1, -1):
        if BUCKET_HI[b] is not None:
            acc = jnp.where(dist <= BUCKET_HI[b], tab_ref[b, c], acc)
    acc = (acc - tab_ref[NUM_BUCKETS - 1, c]) * LOG2E
    if causal:
        acc = jnp.where(k <= q, acc, NEG_INF)
    out_ref[0] = acc


def _toeplitz_bias(tab_pad, rows, cols, off, causal, transposed=False):
    n = tab_pad.shape[1]
    return pl.pallas_call(
        functools.partial(_toeplitz_kernel, off=off, causal=causal, transposed=transposed),
        out_shape=jax.ShapeDtypeStruct((n, rows, cols), F32),
        grid=(n,),
        in_specs=[pl.BlockSpec(memory_space=pltpu.SMEM)],
        out_specs=pl.BlockSpec((1, rows, cols), lambda c: (c, 0, 0)),
        compiler_params=_params(1),
        name="toeplitz_bias",
    )(tab_pad)


_E_GM = 0
_E_DQ = _E_GM + MLA_WIDTH
_E_DK = _E_DQ + DIFF_WIDTH
_E_DV = _E_DK + DIFF_KV_WIDTH
_E_GD = _E_DV + DIFF_KV_WIDTH
_E_CKV = _E_GD + DIFF_WIDTH
_E_CQ = _E_CKV + MLA_KV_LORA
_E_KA = _E_CQ + 2 * LANES
_E_KB = _E_KA + LANES
_E_END = _E_KB + LANES


def _even_in_kernel(x_ref, ng_ref, win_ref, qg_ref, wuq_ref, wukt_ref, kvg_ref, cos_ref, sin_ref,
                    qm_ref, kcat_ref, ckvt_ref, ckv_ref, kr_ref, gm_ref, dq_ref, dk_ref, dkb_ref, dv_ref,
                    dvt_ref, gd_ref):
    h = _rms(x_ref[...], ng_ref[...], NORM_EPS).astype(BF16)
    z = _dot(h, win_ref[...])
    cos2 = cos_ref[...]
    sin2 = sin_ref[...]
    gm_ref[...] = z[:, _E_GM:_E_DQ]
    gd_ref[...] = z[:, _E_GD:_E_CKV]
    dk = z[:, _E_DK:_E_DV]
    dv = z[:, _E_DV:_E_GD]
    dk_ref[...] = dk
    dkb_ref[...] = dk.astype(BF16)
    dv_ref[...] = dv
    dvt_ref[0] = dv.T.astype(BF16)
    lane = lax.broadcasted_iota(jnp.int32, (x_ref.shape[0], LANES), 1)
    dscale = DIFF_HEAD_DIM ** -0.5 * LOG2E
    group = DIFF_HEADS // DIFF_KV_HEADS
    for kv in range(DIFF_KV_HEADS):
        for g in range(group):
            c0 = _E_DQ + (kv * group + g) * 2 * DIFF_HEAD_DIM
            src = z[:, c0:c0 + 2 * DIFF_HEAD_DIM] * dscale
            dq_ref[kv, g] = jnp.where(lane < DIFF_HEAD_DIM, src, 0.0).astype(BF16)
            dq_ref[kv, group + g] = jnp.where(lane >= DIFF_HEAD_DIM, src, 0.0).astype(BF16)
    ckv = _rms(z[:, _E_CKV:_E_CQ], kvg_ref[...], NORM_EPS)
    kr = z[:, _E_KA:_E_KB] * cos2 + z[:, _E_KB:_E_END] * sin2
    ckv_ref[...] = ckv
    kr_ref[...] = kr[:, :MLA_ROPE]
    kcat_ref[...] = jnp.concatenate([ckv, kr], axis=-1).astype(BF16)
    ckvt_ref[0] = ckv.T.astype(BF16)
    cq = _rms(z[:, _E_CQ:_E_CQ + MLA_Q_LORA], qg_ref[...], NORM_EPS).astype(BF16)
    qall = _dot(cq, wuq_ref[...])
    qscale = (MLA_NOPE + MLA_ROPE) ** -0.5 * LOG2E
    nh = MLA_HEADS * LANES
    for hd in range(MLA_HEADS):
        qn = qall[:, hd * LANES:(hd + 1) * LANES].astype(BF16)
        ql = _dot(qn, wukt_ref[hd])
        qr = (qall[:, nh + hd * LANES:nh + (hd + 1) * LANES] * cos2
              + qall[:, 2 * nh + hd * LANES:2 * nh + (hd + 1) * LANES] * sin2)
        qm_ref[hd] = (jnp.concatenate([ql, qr], axis=-1) * qscale).astype(BF16)


def _even_in(x, ng, win, qg, wuq, wukt, kvg, cos2, sin2):
    n, d = x.shape
    tm = min(ROW_TILE, n)
    row = lambda w: pl.BlockSpec((tm, w), lambda i: (i, 0))
    tmajor = lambda w: pl.BlockSpec((1, w, tm), lambda i: (i, 0, 0))
    out_shape = (
        jax.ShapeDtypeStruct((MLA_HEADS, n, 2 * LANES), BF16),
        jax.ShapeDtypeStruct((n, 2 * LANES), BF16),
        jax.ShapeDtypeStruct((n // tm, MLA_KV_LORA, tm), BF16),
        jax.ShapeDtypeStruct((n, MLA_KV_LORA), F32),
        jax.ShapeDtypeStruct((n, MLA_ROPE), F32),
        jax.ShapeDtypeStruct((n, MLA_WIDTH), F32),
        jax.ShapeDtypeStruct((DIFF_KV_HEADS, 4, n, LANES), BF16),
        jax.ShapeDtypeStruct((n, DIFF_KV_WIDTH), F32),
        jax.ShapeDtypeStruct((n, DIFF_KV_WIDTH), BF16),
        jax.ShapeDtypeStruct((n, DIFF_KV_WIDTH), F32),
        jax.ShapeDtypeStruct((n // tm, DIFF_KV_WIDTH, tm), BF16),
        jax.ShapeDtypeStruct((n, DIFF_WIDTH), F32),
    )
    out_specs = (
        pl.BlockSpec((MLA_HEADS, tm, 2 * LANES), lambda i: (0, i, 0)),
        row(2 * LANES), tmajor(MLA_KV_LORA), row(MLA_KV_LORA), row(MLA_ROPE), row(MLA_WIDTH),
        pl.BlockSpec((DIFF_KV_HEADS, 4, tm, LANES), lambda i: (0, 0, i, 0)),
        row(DIFF_KV_WIDTH), row(DIFF_KV_WIDTH), row(DIFF_KV_WIDTH), tmajor(DIFF_KV_WIDTH), row(DIFF_WIDTH),
    )
    return pl.pallas_call(
        _even_in_kernel,
        out_shape=out_shape,
        grid=(n // tm,),
        in_specs=[row(d), _full(ng.shape), _full(win.shape), _full(qg.shape), _full(wuq.shape),
                  _full(wukt.shape), _full(kvg.shape), row(LANES), row(LANES)],
        out_specs=out_specs,
        compiler_params=_params(1),
        name="even_in",
    )(x, ng, win, qg, wuq, wukt, kvg, cos2, sin2)


def _flash_init(m_ref, l_ref, acc_ref):
    m_ref[...] = jnp.full(m_ref.shape, -jnp.inf, F32)
    l_ref[...] = jnp.zeros(l_ref.shape, F32)
    acc_ref[...] = jnp.zeros(acc_ref.shape, F32)


def _flash_update(s, v, m_ref, l_ref, acc_ref):
    m_prev = m_ref[...]
    m_new = jnp.maximum(m_prev, jnp.max(s, axis=-1, keepdims=True))
    alpha = jnp.exp2(m_prev - m_new)
    p = jnp.exp2(s - m_new)
    l_ref[...] = alpha * l_ref[...] + jnp.sum(p, axis=-1, keepdims=True)
    acc_ref[...] = alpha * acc_ref[...] + _dot(p.astype(BF16), v)
    m_ref[...] = m_new


def _flash_update_t(st, vt, m_ref, l_ref, acc_ref):
    m_prev = m_ref[...]
    m_new = jnp.maximum(m_prev, jnp.max(st, axis=0, keepdims=True))
    alpha = jnp.exp2(m_prev - m_new)
    p = jnp.exp2(st - m_new)
    l_ref[...] = alpha * l_ref[...] + jnp.sum(p, axis=0, keepdims=True)
    acc_ref[...] = alpha * acc_ref[...] + _dot(vt, p.astype(BF16))
    m_ref[...] = m_new


def _vt_tiles(vt_ref, j, w, rows):
    tiles = [vt_ref[j + u, rows, :] for u in range(w)]
    return tiles[0] if w == 1 else jnp.concatenate(tiles, axis=1)


def _far_tiles(n_far, logits, update, st0_ref, st1_ref):
    npairs = n_far // 2
    last = jnp.maximum(npairs - 1, 0)

    @pl.when(npairs > 0)
    def _():
        st0_ref[...] = logits(0, 2)

    def body(u, carry):
        a = 2 * u
        cur = st0_ref[...]
        st1_ref[...] = logits(2 * jnp.minimum(a + 1, last), 2)
        update(cur, 2 * a, 2)

        @pl.when(a + 1 < npairs)
        def _():
            cur = st1_ref[...]
            st0_ref[...] = logits(2 * jnp.minimum(a + 2, last), 2)
            update(cur, 2 * (a + 1), 2)

        return carry

    lax.fori_loop(0, (npairs + 1) // 2, body, 0)

    @pl.when(n_far % 2 == 1)
    def _():
        update(logits(n_far - 1, 1), n_far - 1, 1)


def _mla_prompt_kernel(q_ref, k_ref, vt_ref, mask_ref, o_ref, m_ref, l_ref, acc_ref, st0_ref, st1_ref):
    i = pl.program_id(0)
    t = ATT_TILE
    hh = MLA_HEADS // MLA_HEAD_PASSES

    def update(st, j, w):
        _flash_update_t(st, _vt_tiles(vt_ref, j, w, slice(None)), m_ref, l_ref, acc_ref)

    for part in range(MLA_HEAD_PASSES):
        heads = slice(part * hh, (part + 1) * hh)
        q = q_ref[heads].reshape(hh * t, 2 * LANES)
        _flash_init(m_ref, l_ref, acc_ref)

        def logits(j, w, q=q):
            return _dot_nt(k_ref[pl.ds(pl.multiple_of(j * t, t), w * t), :], q)

        _far_tiles(i, logits, update, st0_ref, st1_ref)
        update(logits(i, 1) + mask_ref[...], i, 1)
        o = (acc_ref[...] / l_ref[...]).T
        o_ref[heads] = o.reshape(hh, t, MLA_KV_LORA)


def _mla_prompt(qm, kcat, ckvt, mask):
    s = kcat.shape[0]
    t = ATT_TILE
    rows = (MLA_HEADS // MLA_HEAD_PASSES) * t
    return pl.pallas_call(
        _mla_prompt_kernel,
        out_shape=jax.ShapeDtypeStruct((MLA_HEADS, s, MLA_KV_LORA), F32),
        grid=(s // t,),
        in_specs=[pl.BlockSpec((MLA_HEADS, t, 2 * LANES), lambda i: (0, i, 0)),
                  _resident(kcat.shape), _resident(ckvt.shape), _resident(mask.shape)],
        out_specs=pl.BlockSpec((MLA_HEADS, t, MLA_KV_LORA), lambda i: (0, i, 0)),
        scratch_shapes=[pltpu.VMEM((1, rows), F32), pltpu.VMEM((1, rows), F32),
                        pltpu.VMEM((MLA_KV_LORA, rows), F32),
                        pltpu.VMEM((2 * t, rows), F32), pltpu.VMEM((2 * t, rows), F32)],
        compiler_params=_params(1),
        name="mla_prompt",
    )(qm, kcat, ckvt, mask)


def _diff_lambda(lp, lambda_init):
    a = jnp.sum(lp[0:1] * lp[1:2], axis=-1, keepdims=True)
    b = jnp.sum(lp[2:3] * lp[3:4], axis=-1, keepdims=True)
    return jnp.exp(a) - jnp.exp(b) + lambda_init


def _diff_prompt_kernel(q_ref, k_ref, vt_ref, bias_ref, lp_ref, o_ref, m_ref, l_ref, acc_ref, st0_ref, st1_ref, *,
                        lambda_init):
    i = pl.program_id(0)
    t = ATT_TILE
    group = DIFF_HEADS // DIFF_KV_HEADS
    rows = 2 * group * t
    lam = _diff_lambda(lp_ref[...], lambda_init)
    for kv in range(DIFF_KV_HEADS):
        q = q_ref[kv].reshape(rows, LANES)
        cols = slice(kv * LANES, (kv + 1) * LANES)
        _flash_init(m_ref, l_ref, acc_ref)

        def logits(j, w, q=q, cols=cols):
            return _dot_nt(k_ref[pl.ds(pl.multiple_of(j * t, t), w * t), cols], q)

        def update(st, j, w, cols=cols):
            _flash_update_t(st, _vt_tiles(vt_ref, j, w, cols), m_ref, l_ref, acc_ref)

        _far_tiles(jnp.maximum(i - 1, 0), logits, update, st0_ref, st1_ref)

        @pl.when(i >= 1)
        def _(logits=logits, update=update, kv=kv):
            update(logits(i - 1, 1) + bias_ref[kv, 1], i - 1, 1)

        update(logits(i, 1) + bias_ref[kv, 0], i, 1)
        ot = acc_ref[...] / l_ref[...]
        o = (ot[:, :group * t] - lam * ot[:, group * t:]).T
        for g in range(group):
            hd = kv * group + g
            o_ref[:, hd * LANES:(hd + 1) * LANES] = o[g * t:(g + 1) * t]


def _diff_prompt(dq, dkb, dvt, bias, lp, lambda_init):
    s = dkb.shape[0]
    t = ATT_TILE
    rows = 4 * t
    return pl.pallas_call(
        functools.partial(_diff_prompt_kernel, lambda_init=lambda_init),
        out_shape=jax.ShapeDtypeStruct((s, DIFF_WIDTH), F32),
        grid=(s // t,),
        in_specs=[pl.BlockSpec((DIFF_KV_HEADS, 4, t, LANES), lambda i: (0, 0, i, 0)),
                  _resident(dkb.shape), _resident(dvt.shape), _resident(bias.shape), _full(lp.shape)],
        out_specs=pl.BlockSpec((t, DIFF_WIDTH), lambda i: (i, 0)),
        scratch_shapes=[pltpu.VMEM((1, rows), F32), pltpu.VMEM((1, rows), F32),
                        pltpu.VMEM((LANES, rows), F32),
                        pltpu.VMEM((2 * t, rows), F32), pltpu.VMEM((2 * t, rows), F32)],
        compiler_params=_params(1),
        name="diff_prompt",
    )(dq, dkb, dvt, bias, lp)


_O_Q = 0
_O_K = _O_Q + MOBA_WIDTH
_O_V = _O_K + MOBA_KV_WIDTH
_O_G = _O_V + MOBA_KV_WIDTH
_O_END = _O_G + MOBA_WIDTH


def _mid_kernel(x_ref, olat_ref, wuv_ref, diffo_ref, gm_ref, gd_ref, subg_ref, wout_ref, ng_ref, win_ref,
                x1_ref, q_ref, k_ref, kb_ref, v_ref, vt_ref, g_ref, mean_ref, *, sub_scale):
    mla = jnp.concatenate([_dot(olat_ref[hd].astype(BF16), wuv_ref[hd]) for hd in range(MLA_HEADS)], axis=-1)
    dn = jnp.concatenate(
        [_rms(diffo_ref[:, hd * LANES:(hd + 1) * LANES], subg_ref[...], SUBLN_EPS) * sub_scale
         for hd in range(DIFF_HEADS)], axis=-1)
    mix = jnp.concatenate([mla * _silu(gm_ref[...]), dn * _silu(gd_ref[...])], axis=-1).astype(BF16)
    x1 = x_ref[...] + _dot(mix, wout_ref[...])
    x1_ref[...] = x1
    h = _rms(x1, ng_ref[...], NORM_EPS).astype(BF16)
    z = _dot(h, win_ref[...])
    q_ref[...] = z[:, _O_Q:_O_K]
    k = z[:, _O_K:_O_V]
    v = z[:, _O_V:_O_G]
    k_ref[...] = k
    hot = lax.broadcasted_iota(jnp.int32, (k.shape[0], LANES), 1) == pl.program_id(0)
    hot = jnp.where(hot, 1.0, 0.0)
    kb_ref[...] = jnp.concatenate(
        [piece for kv in range(MOBA_KV_HEADS) for piece in (k[:, kv * LANES:(kv + 1) * LANES], hot)],
        axis=-1).astype(BF16)
    v_ref[...] = v
    vt_ref[0] = v.T.astype(BF16)
    g_ref[...] = z[:, _O_G:_O_END]
    mean_ref[0] = jnp.mean(k, axis=0, keepdims=True)


def _mid(x, olat, wuv, diffo, gm, gd, subg, wout, ng, win, sub_scale):
    n, d = x.shape
    tm = min(ROW_TILE, n)
    row = lambda w: pl.BlockSpec((tm, w), lambda i: (i, 0))
    out_shape = (
        jax.ShapeDtypeStruct((n, d), F32), jax.ShapeDtypeStruct((n, MOBA_WIDTH), F32),
        jax.ShapeDtypeStruct((n, MOBA_KV_WIDTH), F32), jax.ShapeDtypeStruct((n, 2 * MOBA_KV_WIDTH), BF16),
        jax.ShapeDtypeStruct((n, MOBA_KV_WIDTH), F32), jax.ShapeDtypeStruct((n // tm, MOBA_KV_WIDTH, tm), BF16),
        jax.ShapeDtypeStruct((n, MOBA_WIDTH), F32), jax.ShapeDtypeStruct((n // tm, 1, MOBA_KV_WIDTH), F32),
    )
    out_specs = (row(d), row(MOBA_WIDTH), row(MOBA_KV_WIDTH), row(2 * MOBA_KV_WIDTH), row(MOBA_KV_WIDTH),
                 pl.BlockSpec((1, MOBA_KV_WIDTH, tm), lambda i: (i, 0, 0)), row(MOBA_WIDTH),
                 pl.BlockSpec((1, 1, MOBA_KV_WIDTH), lambda i: (i, 0, 0)))
    return pl.pallas_call(
        functools.partial(_mid_kernel, sub_scale=sub_scale),
        out_shape=out_shape,
        grid=(n // tm,),
        in_specs=[row(d), pl.BlockSpec((MLA_HEADS, tm, MLA_KV_LORA), lambda i: (0, i, 0)), _full(wuv.shape),
                  row(DIFF_WIDTH), row(MLA_WIDTH), row(DIFF_WIDTH), _full(subg.shape), _full(wout.shape),
                  _full(ng.shape), _full(win.shape)],
        out_specs=out_specs,
        compiler_params=_params(1),
        name="even_out_odd_in",
    )(x, olat, wuv, diffo, gm, gd, subg, wout, ng, win)


def _topk_mask(gate, ksel):
    lane = lax.broadcasted_iota(jnp.int32, gate.shape, 1).astype(F32)
    sel = jnp.zeros(gate.shape, jnp.bool_)
    for _ in range(ksel):
        mx = jnp.max(gate, axis=-1, keepdims=True)
        idx = jnp.min(jnp.where(gate == mx, lane, float(gate.shape[1])), axis=-1, keepdims=True)
        pick = lane == idx
        sel = jnp.logical_or(sel, pick)
        gate = jnp.where(pick, -jnp.inf, gate)
    return sel


def _moba_prompt_kernel(q_ref, mean_ref, k_ref, vt_ref, bias_ref, o_ref, m_ref, l_ref, acc_ref, st0_ref, st1_ref, *,
                        nblocks):
    i = pl.program_id(0)
    t = ATT_TILE
    nbp = mean_ref.shape[0]
    group = MOBA_HEADS // MOBA_KV_HEADS
    rows = group * t
    scale = MOBA_HEAD_DIM ** -0.5 * LOG2E
    blk = lax.broadcasted_iota(jnp.int32, (rows, nbp), 1)
    for kv in range(MOBA_KV_HEADS):
        cols = slice(kv * LANES, (kv + 1) * LANES)
        qf = jnp.concatenate(
            [q_ref[:, (kv * group + g) * LANES:(kv * group + g + 1) * LANES] for g in range(group)], axis=0)
        gate = lax.dot_general(qf, mean_ref[:, cols], (((1,), (1,)), ((), ())),
                               precision=lax.Precision.HIGHEST, preferred_element_type=F32)
        past = blk < i
        sel = jnp.logical_and(_topk_mask(jnp.where(past, gate, NEG_INF), min(MOBA_TOPK, nblocks)), past)
        q = jnp.concatenate([qf * scale, jnp.where(sel, 0.0, NEG_INF)], axis=-1).astype(BF16)
        _flash_init(m_ref, l_ref, acc_ref)

        def logits(j, w, masked=True, q=q, kv=kv):
            rows_j = pl.ds(pl.multiple_of(j * t, t), w * t)
            k0 = kv * (LANES + nbp)
            if masked:
                return _dot_nt(k_ref[rows_j, k0:k0 + LANES + nbp], q)
            return _dot_nt(k_ref[rows_j, k0:k0 + LANES], q[:, :LANES])

        def update(st, j, w, cols=cols):
            _flash_update_t(st, _vt_tiles(vt_ref, j, w, cols), m_ref, l_ref, acc_ref)

        _far_tiles(jnp.maximum(i - 1, 0), logits, update, st0_ref, st1_ref)

        @pl.when(i >= 1)
        def _(logits=logits, update=update, kv=kv):
            update(logits(i - 1, 1) + bias_ref[kv, 1], i - 1, 1)

        update(logits(i, 1, masked=False) + bias_ref[kv, 0], i, 1)
        o = (acc_ref[...] / l_ref[...]).T
        for g in range(group):
            hd = kv * group + g
            o_ref[:, hd * LANES:(hd + 1) * LANES] = o[g * t:(g + 1) * t]


def _moba_prompt(q, means, kb, vt, bias, nblocks):
    s = kb.shape[0]
    t = ATT_TILE
    rows = (MOBA_HEADS // MOBA_KV_HEADS) * t
    return pl.pallas_call(
        functools.partial(_moba_prompt_kernel, nblocks=nblocks),
        out_shape=jax.ShapeDtypeStruct((s, MOBA_WIDTH), F32),
        grid=(s // t,),
        in_specs=[pl.BlockSpec((t, MOBA_WIDTH), lambda i: (i, 0)), _resident(means.shape),
                  _resident(kb.shape), _resident(vt.shape), _resident(bias.shape)],
        out_specs=pl.BlockSpec((t, MOBA_WIDTH), lambda i: (i, 0)),
        scratch_shapes=[pltpu.VMEM((1, rows), F32), pltpu.VMEM((1, rows), F32),
                        pltpu.VMEM((LANES, rows), F32),
                        pltpu.VMEM((2 * t, rows), F32), pltpu.VMEM((2 * t, rows), F32)],
        compiler_params=_params(1),
        name="moba_prompt",
    )(q, means, kb, vt, bias)


def _final_kernel(x_ref, o_ref, g_ref, wout_ref, fg_ref, y_ref):
    mix = (o_ref[...] * _silu(g_ref[...])).astype(BF16)
    x2 = x_ref[...] + _dot(mix, wout_ref[...])
    y_ref[...] = _rms(x2, fg_ref[...], NORM_EPS)


def _final(x1, o, g, wout, fg):
    n, d = x1.shape
    tm = min(ROW_TILE, n)
    row = lambda w: pl.BlockSpec((tm, w), lambda i: (i, 0))
    return pl.pallas_call(
        _final_kernel,
        out_shape=jax.ShapeDtypeStruct((n, d), F32),
        grid=(n // tm,),
        in_specs=[row(d), row(MOBA_WIDTH), row(MOBA_WIDTH), _full(wout.shape), _full(fg.shape)],
        out_specs=row(d),
        compiler_params=_params(1),
        name="odd_out_final",
    )(x1, o, g, wout, fg)


PAGE_RING = 3


def _fetch_pages(pt_ref, caches, bufs, sems, n):
    b, c = pl.program_id(0), pl.program_id(1)
    steps = pl.num_programs(1)
    total = pl.num_programs(0) * steps
    step = b * steps + c
    ahead = PAGE_RING - 1

    def copies(flat, lookup):
        sslot = lax.rem(flat, PAGE_RING)
        sb, sc = lax.div(flat, steps), lax.rem(flat, steps)
        for p in range(n):
            page = pt_ref[sb, sc * n + p] if lookup else 0
            for cache, buf, sem in zip(caches, bufs, sems):
                yield pltpu.make_async_copy(cache.at[page], buf.at[sslot, p], sem.at[sslot])

    def start(flat):
        @pl.when(flat < total)
        def _():
            for cp in copies(flat, True):
                cp.start()

    @pl.when(step == 0)
    def _():
        for k in range(ahead):
            start(step + k)

    start(step + ahead)
    for cp in copies(step, False):
        cp.wait()
    return lax.rem(step, PAGE_RING)


def _page_scratch(caches, n):
    bufs = [pltpu.VMEM((PAGE_RING, n) + c.shape[1:], c.dtype) for c in caches]
    return bufs + [pltpu.SemaphoreType.DMA((PAGE_RING,)) for _ in caches]


def _hbm_specs(caches):
    return [pl.BlockSpec(memory_space=pl.ANY) for _ in caches]


def _seq_spec(shape):
    return pl.BlockSpec((1,) + shape, lambda b, c, pt: (b,) + (0,) * len(shape))


def _const_spec(a):
    return pl.BlockSpec(a.shape, lambda b, c, pt: (0,) * a.ndim)


def _mla_sample_kernel(pt_ref, q_ref, knew_ref, mask_ref, ckv_hbm, krt_hbm, o_ref, m_ref, l_ref, acc_ref,
                       ckv_buf, krt_buf, ckv_sem, krt_sem, *, n):
    slot = _fetch_pages(pt_ref, (ckv_hbm, krt_hbm), (ckv_buf, krt_buf), (ckv_sem, krt_sem), n)
    c = pl.program_id(1)
    q = q_ref[0]

    @pl.when(c == 0)
    def _():
        _flash_init(m_ref, l_ref, acc_ref)

    ckv = ckv_buf[slot].reshape(n * PAGE_SIZE, MLA_KV_LORA).astype(BF16)
    krt = jnp.concatenate([krt_buf[slot, p].astype(BF16) for p in range(n)], axis=1)
    s = _dot_nt(q[:, :MLA_KV_LORA], ckv) + _dot(q[:, MLA_KV_LORA:MLA_KV_LORA + MLA_ROPE], krt)
    _flash_update(s, ckv, m_ref, l_ref, acc_ref)

    @pl.when(c == pl.num_programs(1) - 1)
    def _():
        knew = knew_ref[0]
        _flash_update(_dot_nt(q, knew) + mask_ref[...], knew[:, :MLA_KV_LORA], m_ref, l_ref, acc_ref)
        o_ref[0] = acc_ref[...] / l_ref[...]


def _mla_sample(page_table, q, knew, mask, cache_ckv, cache_krt):
    nseq, rows, _ = q.shape
    n = min(MLA_PAGES_PER_STEP, page_table.shape[1])
    assert page_table.shape[1] % n == 0
    steps = page_table.shape[1] // n
    caches = (cache_ckv, cache_krt)
    grid_spec = pltpu.PrefetchScalarGridSpec(
        num_scalar_prefetch=1,
        grid=(nseq, steps),
        in_specs=[_seq_spec((rows, 2 * LANES)), _seq_spec((LANES, 2 * LANES)), _const_spec(mask)]
        + _hbm_specs(caches),
        out_specs=_seq_spec((rows, MLA_KV_LORA)),
        scratch_shapes=[pltpu.VMEM((rows, 1), F32), pltpu.VMEM((rows, 1), F32),
                        pltpu.VMEM((rows, MLA_KV_LORA), F32)] + _page_scratch(caches, n),
    )
    return pl.pallas_call(
        functools.partial(_mla_sample_kernel, n=n),
        out_shape=jax.ShapeDtypeStruct((nseq, rows, MLA_KV_LORA), F32),
        grid_spec=grid_spec,
        compiler_params=_params(2),
        name="mla_sample",
    )(page_table, q, knew, mask, *caches)


def _diff_sample_kernel(pt_ref, q_ref, knew_ref, vnew_ref, pmask_ref, blast_ref, bnew_ref, lp_ref, k_hbm, v_hbm,
                        o_ref, m_ref, l_ref, acc_ref, k_buf, v_buf, k_sem, v_sem, *, n, lambda_init):
    slot = _fetch_pages(pt_ref, (k_hbm, v_hbm), (k_buf, v_buf), (k_sem, v_sem), n)
    c = pl.program_id(1)
    last = pl.num_programs(1) - 1
    q = q_ref[0]
    step_rows = n * k_buf.shape[2]

    @pl.when(c == 0)
    def _():
        _flash_init(m_ref, l_ref, acc_ref)

    def pages(bias_ref):
        k = k_buf[slot].reshape(step_rows, LANES).astype(BF16)
        v = v_buf[slot].reshape(step_rows, LANES).astype(BF16)
        _flash_update(_dot_nt(q, k) + bias_ref[...], v, m_ref, l_ref, acc_ref)

    @pl.when(c != last)
    def _():
        pages(pmask_ref)

    @pl.when(c == last)
    def _():
        pages(blast_ref)
        _flash_update(_dot_nt(q, knew_ref[0].astype(BF16)) + bnew_ref[...], vnew_ref[0].astype(BF16),
                      m_ref, l_ref, acc_ref)
        lam = _diff_lambda(lp_ref[...], lambda_init)
        o = acc_ref[...] / l_ref[...]
        per_kv = q.shape[0] // DIFF_KV_HEADS
        half = per_kv // 2
        for kv in range(DIFF_KV_HEADS):
            r0 = kv * per_kv
            o_ref[0, kv] = o[r0:r0 + half] - lam * o[r0 + half:r0 + per_kv]


def _diff_sample(page_table, q, knew, vnew, pmask, blast, bnew, lp, cache_k, cache_v, lambda_init):
    nseq, rows, _ = q.shape
    n = min(PAGES_PER_STEP, page_table.shape[1])
    steps = page_table.shape[1] // n
    prow = cache_k.shape[1]
    caches = (cache_k, cache_v)
    grid_spec = pltpu.PrefetchScalarGridSpec(
        num_scalar_prefetch=1,
        grid=(nseq, steps),
        in_specs=[_seq_spec((rows, LANES)), _seq_spec((prow, LANES)), _seq_spec((prow, LANES)),
                  _const_spec(pmask), _const_spec(blast), _const_spec(bnew), _const_spec(lp)]
        + _hbm_specs(caches),
        out_specs=_seq_spec((DIFF_KV_HEADS, rows // 4, LANES)),
        scratch_shapes=[pltpu.VMEM((rows, 1), F32), pltpu.VMEM((rows, 1), F32), pltpu.VMEM((rows, LANES), F32)]
        + _page_scratch(caches, n),
    )
    return pl.pallas_call(
        functools.partial(_diff_sample_kernel, n=n, lambda_init=lambda_init),
        out_shape=jax.ShapeDtypeStruct((nseq, DIFF_KV_HEADS, rows // 4, LANES), F32),
        grid_spec=grid_spec,
        compiler_params=_params(2),
        name="diff_sample",
    )(page_table, q, knew, vnew, pmask, blast, bnew, lp, *caches)


def _moba_sample_kernel(pt_ref, q_ref, knew_ref, vnew_ref, pmask_ref, blast_ref, bnew_ref, k_hbm, v_hbm,
                        o_ref, gate_ref, mb_ref, lb_ref, ob_ref, k_buf, v_buf, k_sem, v_sem, *, n, nblocks):
    slot = _fetch_pages(pt_ref, (k_hbm, v_hbm), (k_buf, v_buf), (k_sem, v_sem), n)
    c = pl.program_id(1)
    last = pl.num_programs(1) - 1
    ppb = MOBA_BLOCK // PAGE_SIZE
    bps = n // ppb
    prow = k_buf.shape[2]
    brow = ppb * prow
    rows = q_ref.shape[1]
    qf = q_ref[0]
    qb = (qf * (MOBA_HEAD_DIM ** -0.5 * LOG2E)).astype(BF16)
    lane = lax.broadcasted_iota(jnp.int32, (rows, nblocks), 1)
    row_kv = lax.broadcasted_iota(jnp.int32, (rows, LANES), 0) // (rows // MOBA_KV_HEADS)
    key_kv = lax.broadcasted_iota(jnp.int32, (brow, LANES), 0) % MOBA_KV_HEADS

    @pl.when(c == 0)
    def _():
        for ref in (gate_ref, mb_ref, lb_ref):
            ref[...] = jnp.zeros(ref.shape, F32)

    def put(ref, blk, col):
        ref[...] = jnp.where(lane == blk, col, ref[...])

    def blocks(last_bias_ref):
        kf = k_buf[slot].reshape(n * prow, LANES)
        vb = v_buf[slot].reshape(n * prow, LANES).astype(BF16)
        s_all = _dot_nt(qb, kf.astype(BF16))
        for bi in range(bps):
            blk = c * bps + bi
            kblk = kf[bi * brow:(bi + 1) * brow]
            kmean = jnp.zeros((rows, LANES), F32)
            for kv in range(MOBA_KV_HEADS):
                mean_kv = jnp.sum(jnp.where(key_kv == kv, kblk, 0.0), axis=0, keepdims=True) * (1.0 / MOBA_BLOCK)
                kmean = jnp.where(row_kv == kv, mean_kv, kmean)
            put(gate_ref, blk, jnp.sum(qf * kmean, axis=-1, keepdims=True))
            bias_ref = last_bias_ref if bi == bps - 1 else pmask_ref
            s = s_all[:, bi * brow:(bi + 1) * brow] + bias_ref[...]
            m = jnp.max(s, axis=-1, keepdims=True)
            p = jnp.exp2(s - m)
            put(mb_ref, blk, m)
            put(lb_ref, blk, jnp.sum(p, axis=-1, keepdims=True))
            ob_ref[blk] = _dot(p.astype(BF16), vb[bi * brow:(bi + 1) * brow])

    @pl.when(c != last)
    def _():
        blocks(pmask_ref)

    @pl.when(c == last)
    def _():
        blocks(blast_ref)
        s = _dot_nt(qb, knew_ref[0].astype(BF16)) + bnew_ref[...]
        m_new = jnp.max(s, axis=-1, keepdims=True)
        p = jnp.exp2(s - m_new)
        l_new = jnp.sum(p, axis=-1, keepdims=True)
        o_new = _dot(p.astype(BF16), vnew_ref[0].astype(BF16))
        sel = _topk_mask(gate_ref[...], min(MOBA_TOPK, nblocks))
        mb = mb_ref[...]
        m_all = jnp.maximum(jnp.max(jnp.where(sel, mb, -jnp.inf), axis=-1, keepdims=True), m_new)
        w = jnp.where(sel, jnp.exp2(mb - m_all), 0.0)
        w_new = jnp.exp2(m_new - m_all)
        den = jnp.sum(w * lb_ref[...], axis=-1, keepdims=True) + w_new * l_new
        num = w_new * o_new
        for blk in range(nblocks):
            num = num + w[:, blk:blk + 1] * ob_ref[blk]
        o_ref[0] = num / den


def _moba_sample(page_table, q, knew, vnew, pmask, blast, bnew, cache_k, cache_v):
    nseq, rows, _ = q.shape
    ppb = MOBA_BLOCK // PAGE_SIZE
    nblocks = page_table.shape[1] // ppb
    n = min(PAGES_PER_STEP, page_table.shape[1])
    steps = page_table.shape[1] // n
    prow = cache_k.shape[1]
    caches = (cache_k, cache_v)
    grid_spec = pltpu.PrefetchScalarGridSpec(
        num_scalar_prefetch=1,
        grid=(nseq, steps),
        in_specs=[_seq_spec((rows, LANES)), _seq_spec((prow, LANES)), _seq_spec((prow, LANES)),
                  _const_spec(pmask), _const_spec(blast), _const_spec(bnew)]
        + _hbm_specs(caches),
        out_specs=_seq_spec((rows, LANES)),
        scratch_shapes=[pltpu.VMEM((rows, nblocks), F32), pltpu.VMEM((rows, nblocks), F32),
                        pltpu.VMEM((rows, nblocks), F32), pltpu.VMEM((nblocks, rows, LANES), F32)]
        + _page_scratch(caches, n),
    )
    return pl.pallas_call(
        functools.partial(_moba_sample_kernel, n=n, nblocks=nblocks),
        out_shape=jax.ShapeDtypeStruct((nseq, rows, LANES), F32),
        grid_spec=grid_spec,
        compiler_params=_params(2),
        name="moba_sample",
    )(page_table, q, knew, vnew, pmask, blast, bnew, *caches)


def _rope_tables(pos):
    half = MLA_ROPE // 2
    inv = ROPE_BASE ** (-jnp.arange(half, dtype=F32) / half)
    ang = pos.astype(F32)[:, None] * inv
    pad = jnp.zeros((pos.shape[0], LANES - 2 * half), F32)
    cos = jnp.cos(ang)
    sin = jnp.sin(ang)
    return jnp.concatenate([cos, cos, pad], axis=-1), jnp.concatenate([sin, sin, pad], axis=-1)


def _lane_pad(w, width):
    return jnp.pad(w, ((0, 0), (0, width - w.shape[1])))


def _even_weights(w_in, w_uq, w_uk, w_uv):
    half = MLA_ROPE // 2
    sizes = [MLA_Q_LORA, MLA_KV_LORA, MLA_ROPE, MLA_WIDTH, DIFF_WIDTH, DIFF_KV_WIDTH, DIFF_KV_WIDTH]
    cq, ckv, kr, gm, dq, dk, dv, gd = jnp.split(w_in, np.cumsum(sizes), axis=1)
    ka = _lane_pad(kr, LANES)
    kb = _lane_pad(jnp.concatenate([-kr[:, half:], kr[:, :half]], axis=1), LANES)
    win = jnp.concatenate([gm, dq, dk, dv, gd, ckv, _lane_pad(cq, 2 * LANES), ka, kb], axis=1).astype(BF16)
    per_head = lambda w: jnp.pad(w, ((0, 0), (0, 0), (0, LANES - w.shape[2]))).reshape(w.shape[0], MLA_HEADS * LANES)
    nope = per_head(w_uq[:, :, :MLA_NOPE])
    ra = per_head(w_uq[:, :, MLA_NOPE:])
    rb = per_head(jnp.concatenate([-w_uq[:, :, MLA_NOPE + half:], w_uq[:, :, MLA_NOPE:MLA_NOPE + half]], axis=2))
    wuq = jnp.concatenate([nope, ra, rb], axis=1).astype(BF16)
    wukt = jnp.pad(jnp.transpose(w_uk, (1, 2, 0)), ((0, 0), (0, LANES - MLA_NOPE), (0, 0))).astype(BF16)
    wuv = jnp.transpose(w_uv, (1, 0, 2)).astype(BF16)
    return win, wuq, wukt, wuv


def _interleave_kv(tile):
    nkv, r, j = tile.shape
    own = jnp.eye(nkv, dtype=jnp.bool_)[:, None, None, :]
    return jnp.where(own, tile[..., None], NEG_INF).reshape(nkv * r, j * nkv)


def kernel(x_prompt, x_sample, cache_mla_ckv, cache_mla_krope, cache_diff_k, cache_diff_v, cache_moba_k, cache_moba_v, page_table, norm_g, final_norm_g, rel_bias, w_in_even, mla_q_norm_g, mla_w_uq, mla_kv_norm_g, mla_w_uk, mla_w_uv, diff_lambda, diff_subln_g, w_out_even, w_in_odd, w_out_odd):
    _, s_len, d_model = x_prompt.shape
    n_dec, t_len, _ = x_sample.shape
    n_pages = page_table.shape[1]
    past_len = n_pages * PAGE_SIZE
    n_smp = n_dec * t_len
    t = ATT_TILE
    n_step = min(PAGES_PER_STEP, n_pages)
    assert norm_g.shape[0] == 2 and x_prompt.shape[0] == 1
    assert ROW_TILE == MOBA_BLOCK == ATT_TILE
    assert s_len % t == 0 and n_smp % min(ROW_TILE, n_smp) == 0 and past_len % MOBA_BLOCK == 0
    assert n_pages % n_step == 0 and n_step % (MOBA_BLOCK // PAGE_SIZE) == 0 and t_len <= 8
    assert FAR_DIST <= PAGE_SIZE + 1 and FAR_DIST <= t + 1
    dgroup = DIFF_HEADS // DIFF_KV_HEADS
    mgroup = MOBA_HEADS // MOBA_KV_HEADS
    nblocks_p = s_len // MOBA_BLOCK
    assert nblocks_p <= LANES
    lambda_init = 0.8 - 0.6 * math.exp(-0.3 * 0)

    tab = jnp.pad(rel_bias.astype(F32), ((0, 0), (0, 8)))
    zero_col = rel_bias.shape[1]
    band0 = _toeplitz_bias(tab, t, t, 0, True, transposed=True)
    band1 = _toeplitz_bias(tab, t, t, t, False, transposed=True)
    s_last_page = _toeplitz_bias(tab, 8, PAGE_SIZE, PAGE_SIZE, False)
    s_last_block = _toeplitz_bias(tab, 8, MOBA_BLOCK, MOBA_BLOCK, False)
    s_new = _toeplitz_bias(tab, 8, LANES, 0, True)
    diff_cols = [[m * DIFF_HEADS + kv * dgroup + g for m in range(2) for g in range(dgroup)]
                 for kv in range(DIFF_KV_HEADS)]
    moba_cols = [[kv * mgroup + g for g in range(mgroup)] for kv in range(MOBA_KV_HEADS)]
    lanes_of = lambda tb, cols: jnp.stack([jnp.concatenate([tb[c] for c in cl], axis=1) for cl in cols])
    rows_of = lambda tb, cols: jnp.stack([jnp.concatenate([tb[c, :t_len] for c in cl], axis=0) for cl in cols])
    diff_band = jnp.stack([lanes_of(band0, diff_cols), lanes_of(band1, diff_cols)], axis=1)
    moba_band = jnp.stack([lanes_of(band0, moba_cols), lanes_of(band1, moba_cols)], axis=1)
    mla_mask = jnp.tile(band0[zero_col], (1, MLA_HEADS // MLA_HEAD_PASSES))
    mla_new_mask = jnp.tile(s_new[zero_col, :t_len], (MLA_HEADS, 1))
    drows = 2 * dgroup * t_len
    diff_pmask = _interleave_kv(jnp.zeros((DIFF_KV_HEADS, drows, n_step * PAGE_SIZE), F32))
    diff_last = jnp.concatenate([diff_pmask[:, :(n_step - 1) * 2 * PAGE_SIZE],
                                 _interleave_kv(rows_of(s_last_page, diff_cols))], axis=1)
    diff_new = _interleave_kv(rows_of(s_new, diff_cols))
    moba_pmask = _interleave_kv(jnp.zeros((MOBA_KV_HEADS, mgroup * t_len, MOBA_BLOCK), F32))
    moba_last = _interleave_kv(rows_of(s_last_block, moba_cols))
    moba_new = _interleave_kv(rows_of(s_new, moba_cols))

    win_e, wuq, wukt, wuv = _even_weights(w_in_even[0], mla_w_uq[0], mla_w_uk[0], mla_w_uv[0])
    wout_e = w_out_even[0].astype(BF16)
    win_o = w_in_odd[0].astype(BF16)
    wout_o = w_out_odd[0].astype(BF16)
    ng0, ng1, fg = norm_g[0][None], norm_g[1][None], final_norm_g[None]
    qg, kvg, subg = mla_q_norm_g[0][None], mla_kv_norm_g[0][None], diff_subln_g[0][None]
    lp = diff_lambda[0].astype(F32)

    xp = x_prompt[0]
    xs = x_sample.reshape(n_smp, d_model)
    cos_p, sin_p = _rope_tables(jnp.arange(s_len))
    cos_s, sin_s = _rope_tables(jnp.tile(past_len + jnp.arange(t_len), n_dec))

    krt_cache = jnp.swapaxes(cache_mla_krope, 2, 3)
    kv_rows = lambda a: a[0].reshape(a.shape[1], a.shape[2] * a.shape[3], a.shape[4])
    new_rows = lambda a, nkv: jnp.pad(a.reshape(n_dec, t_len * nkv, LANES),
                                      ((0, 0), (0, (PAGE_SIZE - t_len) * nkv), (0, 0)))

    (qm_p, kcat_p, ckvt_p, ckv_p, kr_p, gm_p, dq_p, dk_p, dkb_p, dv_p, dvt_p, gd_p) = _even_in(
        xp, ng0, win_e, qg, wuq, wukt, kvg, cos_p, sin_p)
    olat_p = _mla_prompt(qm_p, kcat_p, ckvt_p, mla_mask)
    diffo_p = _diff_prompt(dq_p, dkb_p, dvt_p, diff_band, lp, lambda_init)
    (x1_p, q_p, k_p, kb_p, v_p, vt_p, g_p, means_p) = _mid(
        xp, olat_p, wuv, diffo_p, gm_p, gd_p, subg, wout_e, ng1, win_o, 1.0 - lambda_init)

    (qm_s, kcat_s, _, ckv_s, kr_s, gm_s, dq_s, dk_s, _, dv_s, _, gd_s) = _even_in(
        xs, ng0, win_e, qg, wuq, wukt, kvg, cos_s, sin_s)
    q_mla = qm_s.reshape(MLA_HEADS, n_dec, t_len, 2 * LANES).transpose(1, 0, 2, 3).reshape(
        n_dec, MLA_HEADS * t_len, 2 * LANES)
    knew_mla = jnp.pad(kcat_s.reshape(n_dec, t_len, 2 * LANES), ((0, 0), (0, LANES - t_len), (0, 0)))
    olat_s = _mla_sample(page_table, q_mla, knew_mla, mla_new_mask, cache_mla_ckv[0], krt_cache[0])
    olat_s = olat_s.reshape(n_dec, MLA_HEADS, t_len, MLA_KV_LORA).transpose(1, 0, 2, 3).reshape(
        MLA_HEADS, n_smp, MLA_KV_LORA)
    q_diff = dq_s.reshape(DIFF_KV_HEADS, 4, n_dec, t_len, LANES).transpose(2, 0, 1, 3, 4).reshape(
        n_dec, DIFF_KV_HEADS * drows, LANES)
    diffo_s = _diff_sample(page_table, q_diff, new_rows(dk_s, DIFF_KV_HEADS), new_rows(dv_s, DIFF_KV_HEADS),
                           diff_pmask, diff_last, diff_new, lp, kv_rows(cache_diff_k), kv_rows(cache_diff_v),
                           lambda_init)
    diffo_s = diffo_s.reshape(n_dec, DIFF_KV_HEADS, dgroup, t_len, LANES).transpose(0, 3, 1, 2, 4).reshape(
        n_smp, DIFF_WIDTH)
    (x1_s, q_s, k_s, _, v_s, _, g_s, _) = _mid(
        xs, olat_s, wuv, diffo_s, gm_s, gd_s, subg, wout_e, ng1, win_o, 1.0 - lambda_init)

    means = means_p.reshape(nblocks_p, MOBA_KV_WIDTH)
    means = jnp.pad(means, ((0, -nblocks_p % LANES), (0, 0)))
    o_p = _moba_prompt(q_p, means, kb_p, vt_p, moba_band, nblocks_p)
    y_p = _final(x1_p, o_p, g_p, wout_o, fg)

    q_moba = q_s.reshape(n_dec, t_len, MOBA_KV_HEADS, mgroup, LANES).transpose(0, 2, 3, 1, 4).reshape(
        n_dec, MOBA_HEADS * t_len, LANES)
    o_s = _moba_sample(page_table, q_moba, new_rows(k_s, MOBA_KV_HEADS), new_rows(v_s, MOBA_KV_HEADS),
                       moba_pmask, moba_last, moba_new, kv_rows(cache_moba_k), kv_rows(cache_moba_v))
    o_s = o_s.reshape(n_dec, MOBA_KV_HEADS, mgroup, t_len, LANES).transpose(0, 3, 1, 2, 4).reshape(
        n_smp, MOBA_WIDTH)
    y_s = _final(x1_s, o_s, g_s, wout_o, fg)

    kv4 = lambda a, b, s: a.reshape(1, b, s, 2, LANES)
    return (y_p[None], y_s.reshape(n_dec, t_len, d_model),
            ckv_p.reshape(1, 1, s_len, MLA_KV_LORA), kr_p.reshape(1, 1, s_len, MLA_ROPE),
            kv4(dk_p, 1, s_len), kv4(dv_p, 1, s_len), kv4(k_p, 1, s_len), kv4(v_p, 1, s_len),
            ckv_s.reshape(1, n_dec, t_len, MLA_KV_LORA), kr_s.reshape(1, n_dec, t_len, MLA_ROPE),
            kv4(dk_s, n_dec, t_len), kv4(dv_s, n_dec, t_len), kv4(k_s, n_dec, t_len), kv4(v_s, n_dec, t_len))
```

```python
import functools
import math

import jax
import jax.numpy as jnp
import numpy as np
from jax import lax
from jax.experimental import pallas as pl
from jax.experimental.pallas import tpu as pltpu

F32 = jnp.float32
BF16 = jnp.bfloat16

PAGE_SIZE = 128
MLA_HEADS = 8
MLA_Q_LORA = 192
MLA_KV_LORA = 128
MLA_NOPE = 64
MLA_ROPE = 32
MLA_V = 64
ROPE_BASE = 10000.0
DIFF_HEADS = 4
DIFF_KV_HEADS = 2
DIFF_HEAD_DIM = 64
SUBLN_EPS = 1e-5
MOBA_HEADS = 8
MOBA_KV_HEADS = 2
MOBA_HEAD_DIM = 128
MOBA_BLOCK = 256
MOBA_TOPK = 3
NUM_BUCKETS = 32
REL_MAX_DISTANCE = 128
REL_MAX_EXACT = NUM_BUCKETS // 2
NORM_EPS = 1e-6
NEG_INF = -1e30
LOG2E = math.log2(math.e)

LANES = 128
ROW_TILE = 256
ATT_TILE = 256
PAGES_PER_STEP = 16
MLA_PAGES_PER_STEP = 64
MLA_HEAD_PASSES = 1
VMEM_LIMIT = 56 * 1024 * 1024

MLA_WIDTH = MLA_HEADS * MLA_V
DIFF_WIDTH = DIFF_HEADS * 2 * DIFF_HEAD_DIM
DIFF_KV_WIDTH = DIFF_KV_HEADS * 2 * DIFF_HEAD_DIM
MOBA_WIDTH = MOBA_HEADS * MOBA_HEAD_DIM
MOBA_KV_WIDTH = MOBA_KV_HEADS * MOBA_HEAD_DIM


def _bucket_upper_bounds():
    d = np.arange(0, 4 * REL_MAX_DISTANCE)
    v = np.log(np.maximum(d, 1) / REL_MAX_EXACT) / math.log(REL_MAX_DISTANCE / REL_MAX_EXACT)
    v = v * (NUM_BUCKETS - REL_MAX_EXACT)
    frac = np.abs(v[REL_MAX_EXACT + 1:REL_MAX_DISTANCE] - np.round(v[REL_MAX_EXACT + 1:REL_MAX_DISTANCE]))
    assert frac.min() > 1e-3
    b = np.where(d < REL_MAX_EXACT, d, np.minimum(REL_MAX_EXACT + np.floor(v).astype(np.int64), NUM_BUCKETS - 1))
    hi = [int(d[b == k].max()) if np.any(b == k) else None for k in range(NUM_BUCKETS)]
    far = int(hi[NUM_BUCKETS - 2]) + 1
    return hi, far


BUCKET_HI, FAR_DIST = _bucket_upper_bounds()


def _dot(a, b):
    return jnp.dot(a, b, preferred_element_type=F32)


def _dot_nt(a, b):
    return lax.dot_general(a, b, (((1,), (1,)), ((), ())), preferred_element_type=F32)


def _rms(x, g, eps):
    return x * lax.rsqrt(jnp.mean(x * x, axis=-1, keepdims=True) + eps) * g


def _silu(g):
    return g / (1.0 + jnp.exp(-g))


def _full(shape):
    nd = len(shape)
    return pl.BlockSpec(shape, lambda *_: (0,) * nd)


def _resident(shape):
    nd = len(shape)
    return pl.BlockSpec(shape, lambda *_: (0,) * nd, pipeline_mode=pl.Buffered(1))


def _params(n_axes):
    return pltpu.CompilerParams(dimension_semantics=("arbitrary",) * n_axes, vmem_limit_bytes=VMEM_LIMIT)


def _toeplitz_kernel(tab_ref, out_ref, *, off, causal, transposed):
    c = pl.program_id(0)
    shape = out_ref.shape[1:]
    r = lax.broadcasted_iota(jnp.int32, shape, 0)
    s = lax.broadcasted_iota(jnp.int32, shape, 1)
    q, k = (s, r) if transposed else (r, s)
    dist = off + q - k
    acc = jnp.zeros(shape, F32) + tab_ref[NUM_BUCKETS - 1, c]
    for b in range(NUM_BUCKETS - 2, -1, -1):
        if BUCKET_HI[b] is not None:
            acc = jnp.where(dist <= BUCKET_HI[b], tab_ref[b, c], acc)
    acc = (acc - tab_ref[NUM_BUCKETS - 1, c]) * LOG2E
    if causal:
        acc = jnp.where(k <= q, acc, NEG_INF)
    out_ref[0] = acc


def _toeplitz_bias(tab_pad, rows, cols, off, causal, transposed=False):
    n = tab_pad.shape[1]
    return pl.pallas_call(
        functools.partial(_toeplitz_kernel, off=off, causal=causal, transposed=transposed),
        out_shape=jax.ShapeDtypeStruct((n, rows, cols), F32),
        grid=(n,),
        in_specs=[pl.BlockSpec(memory_space=pltpu.SMEM)],
        out_specs=pl.BlockSpec((1, rows, cols), lambda c: (c, 0, 0)),
        compiler_params=_params(1),
        name="toeplitz_bias",
    )(tab_pad)


_E_GM = 0
_E_DQ = _E_GM + MLA_WIDTH
_E_DK = _E_DQ + DIFF_WIDTH
_E_DV = _E_DK + DIFF_KV_WIDTH
_E_GD = _E_DV + DIFF_KV_WIDTH
_E_CKV = _E_GD + DIFF_WIDTH
_E_CQ = _E_CKV + MLA_KV_LORA
_E_KA = _E_CQ + 2 * LANES
_E_KB = _E_KA + LANES
_E_END = _E_KB + LANES


def _even_in_kernel(x_ref, ng_ref, win_ref, qg_ref, wuq_ref, wukt_ref, kvg_ref, cos_ref, sin_ref,
                    qm_ref, kcat_ref, ckvt_ref, ckv_ref, kr_ref, gm_ref, dq_ref, dk_ref, dkb_ref, dv_ref,
                    dvt_ref, gd_ref):
    h = _rms(x_ref[...], ng_ref[...], NORM_EPS).astype(BF16)
    z = _dot(h, win_ref[...])
    cos2 = cos_ref[...]
    sin2 = sin_ref[...]
    gm_ref[...] = z[:, _E_GM:_E_DQ]
    gd_ref[...] = z[:, _E_GD:_E_CKV]
    dk = z[:, _E_DK:_E_DV]
    dv = z[:, _E_DV:_E_GD]
    dk_ref[...] = dk
    dkb_ref[...] = dk.astype(BF16)
    dv_ref[...] = dv
    dvt_ref[0] = dv.T.astype(BF16)
    lane = lax.broadcasted_iota(jnp.int32, (x_ref.shape[0], LANES), 1)
    dscale = DIFF_HEAD_DIM ** -0.5 * LOG2E
    group = DIFF_HEADS // DIFF_KV_HEADS
    for kv in range(DIFF_KV_HEADS):
        for g in range(group):
            c0 = _E_DQ + (kv * group + g) * 2 * DIFF_HEAD_DIM
            src = z[:, c0:c0 + 2 * DIFF_HEAD_DIM] * dscale
            dq_ref[kv, g] = jnp.where(lane < DIFF_HEAD_DIM, src, 0.0).astype(BF16)
            dq_ref[kv, group + g] = jnp.where(lane >= DIFF_HEAD_DIM, src, 0.0).astype(BF16)
    ckv = _rms(z[:, _E_CKV:_E_CQ], kvg_ref[...], NORM_EPS)
    kr = z[:, _E_KA:_E_KB] * cos2 + z[:, _E_KB:_E_END] * sin2
    ckv_ref[...] = ckv
    kr_ref[...] = kr[:, :MLA_ROPE]
    kcat_ref[...] = jnp.concatenate([ckv, kr], axis=-1).astype(BF16)
    ckvt_ref[0] = ckv.T.astype(BF16)
    cq = _rms(z[:, _E_CQ:_E_CQ + MLA_Q_LORA], qg_ref[...], NORM_EPS).astype(BF16)
    qall = _dot(cq, wuq_ref[...])
    qscale = (MLA_NOPE + MLA_ROPE) ** -0.5 * LOG2E
    nh = MLA_HEADS * LANES
    for hd in range(MLA_HEADS):
        qn = qall[:, hd * LANES:(hd + 1) * LANES].astype(BF16)
        ql = _dot(qn, wukt_ref[hd])
        qr = (qall[:, nh + hd * LANES:nh + (hd + 1) * LANES] * cos2
              + qall[:, 2 * nh + hd * LANES:2 * nh + (hd + 1) * LANES] * sin2)
        qm_ref[hd] = (jnp.concatenate([ql, qr], axis=-1) * qscale).astype(BF16)


def _even_in(x, ng, win, qg, wuq, wukt, kvg, cos2, sin2):
    n, d = x.shape
    tm = min(ROW_TILE, n)
    row = lambda w: pl.BlockSpec((tm, w), lambda i: (i, 0))
    tmajor = lambda w: pl.BlockSpec((1, w, tm), lambda i: (i, 0, 0))
    out_shape = (
        jax.ShapeDtypeStruct((MLA_HEADS, n, 2 * LANES), BF16),
        jax.ShapeDtypeStruct((n, 2 * LANES), BF16),
        jax.ShapeDtypeStruct((n // tm, MLA_KV_LORA, tm), BF16),
        jax.ShapeDtypeStruct((n, MLA_KV_LORA), F32),
        jax.ShapeDtypeStruct((n, MLA_ROPE), F32),
        jax.ShapeDtypeStruct((n, MLA_WIDTH), F32),
        jax.ShapeDtypeStruct((DIFF_KV_HEADS, 4, n, LANES), BF16),
        jax.ShapeDtypeStruct((n, DIFF_KV_WIDTH), F32),
        jax.ShapeDtypeStruct((n, DIFF_KV_WIDTH), BF16),
        jax.ShapeDtypeStruct((n, DIFF_KV_WIDTH), F32),
        jax.ShapeDtypeStruct((n // tm, DIFF_KV_WIDTH, tm), BF16),
        jax.ShapeDtypeStruct((n, DIFF_WIDTH), F32),
    )
    out_specs = (
        pl.BlockSpec((MLA_HEADS, tm, 2 * LANES), lambda i: (0, i, 0)),
        row(2 * LANES), tmajor(MLA_KV_LORA), row(MLA_KV_LORA), row(MLA_ROPE), row(MLA_WIDTH),
        pl.BlockSpec((DIFF_KV_HEADS, 4, tm, LANES), lambda i: (0, 0, i, 0)),
        row(DIFF_KV_WIDTH), row(DIFF_KV_WIDTH), row(DIFF_KV_WIDTH), tmajor(DIFF_KV_WIDTH), row(DIFF_WIDTH),
    )
    return pl.pallas_call(
        _even_in_kernel,
        out_shape=out_shape,
        grid=(n // tm,),
        in_specs=[row(d), _full(ng.shape), _full(win.shape), _full(qg.shape), _full(wuq.shape),
                  _full(wukt.shape), _full(kvg.shape), row(LANES), row(LANES)],
        out_specs=out_specs,
        compiler_params=_params(1),
        name="even_in",
    )(x, ng, win, qg, wuq, wukt, kvg, cos2, sin2)


def _flash_init(m_ref, l_ref, acc_ref):
    m_ref[...] = jnp.full(m_ref.shape, -jnp.inf, F32)
    l_ref[...] = jnp.zeros(l_ref.shape, F32)
    acc_ref[...] = jnp.zeros(acc_ref.shape, F32)


def _flash_update(s, v, m_ref, l_ref, acc_ref):
    m_prev = m_ref[...]
    m_new = jnp.maximum(m_prev, jnp.max(s, axis=-1, keepdims=True))
    alpha = jnp.exp2(m_prev - m_new)
    p = jnp.exp2(s - m_new)
    l_ref[...] = alpha * l_ref[...] + jnp.sum(p, axis=-1, keepdims=True)
    acc_ref[...] = alpha * acc_ref[...] + _dot(p.astype(BF16), v)
    m_ref[...] = m_new


def _flash_update_t(st, vt, m_ref, l_ref, acc_ref):
    m_prev = m_ref[...]
    m_new = jnp.maximum(m_prev, jnp.max(st, axis=0, keepdims=True))
    alpha = jnp.exp2(m_prev - m_new)
    p = jnp.exp2(st - m_new)
    l_ref[...] = alpha * l_ref[...] + jnp.sum(p, axis=0, keepdims=True)
    acc_ref[...] = alpha * acc_ref[...] + _dot(vt, p.astype(BF16))
    m_ref[...] = m_new


def _vt_tiles(vt_ref, j, w, rows):
    tiles = [vt_ref[j + u, rows, :] for u in range(w)]
    return tiles[0] if w == 1 else jnp.concatenate(tiles, axis=1)


def _far_tiles(n_far, logits, update, st0_ref, st1_ref):
    npairs = n_far // 2
    last = jnp.maximum(npairs - 1, 0)

    @pl.when(npairs > 0)
    def _():
        st0_ref[...] = logits(0, 2)

    def body(u, carry):
        a = 2 * u
        cur = st0_ref[...]
        st1_ref[...] = logits(2 * jnp.minimum(a + 1, last), 2)
        update(cur, 2 * a, 2)

        @pl.when(a + 1 < npairs)
        def _():
            cur = st1_ref[...]
            st0_ref[...] = logits(2 * jnp.minimum(a + 2, last), 2)
            update(cur, 2 * (a + 1), 2)

        return carry

    lax.fori_loop(0, (npairs + 1) // 2, body, 0)

    @pl.when(n_far % 2 == 1)
    def _():
        update(logits(n_far - 1, 1), n_far - 1, 1)


def _mla_prompt_kernel(q_ref, k_ref, vt_ref, mask_ref, o_ref, m_ref, l_ref, acc_ref, st0_ref, st1_ref):
    i = pl.program_id(0)
    t = ATT_TILE
    hh = MLA_HEADS // MLA_HEAD_PASSES

    def update(st, j, w):
        _flash_update_t(st, _vt_tiles(vt_ref, j, w, slice(None)), m_ref, l_ref, acc_ref)

    for part in range(MLA_HEAD_PASSES):
        heads = slice(part * hh, (part + 1) * hh)
        q = q_ref[heads].reshape(hh * t, 2 * LANES)
        _flash_init(m_ref, l_ref, acc_ref)

        def logits(j, w, q=q):
            return _dot_nt(k_ref[pl.ds(pl.multiple_of(j * t, t), w * t), :], q)

        _far_tiles(i, logits, update, st0_ref, st1_ref)
        update(logits(i, 1) + mask_ref[...], i, 1)
        o = (acc_ref[...] / l_ref[...]).T
        o_ref[heads] = o.reshape(hh, t, MLA_KV_LORA)


def _mla_prompt(qm, kcat, ckvt, mask):
    s = kcat.shape[0]
    t = ATT_TILE
    rows = (MLA_HEADS // MLA_HEAD_PASSES) * t
    return pl.pallas_call(
        _mla_prompt_kernel,
        out_shape=jax.ShapeDtypeStruct((MLA_HEADS, s, MLA_KV_LORA), F32),
        grid=(s // t,),
        in_specs=[pl.BlockSpec((MLA_HEADS, t, 2 * LANES), lambda i: (0, i, 0)),
                  _resident(kcat.shape), _resident(ckvt.shape), _resident(mask.shape)],
        out_specs=pl.BlockSpec((MLA_HEADS, t, MLA_KV_LORA), lambda i: (0, i, 0)),
        scratch_shapes=[pltpu.VMEM((1, rows), F32), pltpu.VMEM((1, rows), F32),
                        pltpu.VMEM((MLA_KV_LORA, rows), F32),
                        pltpu.VMEM((2 * t, rows), F32), pltpu.VMEM((2 * t, rows), F32)],
        compiler_params=_params(1),
        name="mla_prompt",
    )(qm, kcat, ckvt, mask)


def _diff_lambda(lp, lambda_init):
    a = jnp.sum(lp[0:1] * lp[1:2], axis=-1, keepdims=True)
    b = jnp.sum(lp[2:3] * lp[3:4], axis=-1, keepdims=True)
    return jnp.exp(a) - jnp.exp(b) + lambda_init


def _diff_prompt_kernel(q_ref, k_ref, vt_ref, bias_ref, lp_ref, o_ref, m_ref, l_ref, acc_ref, st0_ref, st1_ref, *,
                        lambda_init):
    i = pl.program_id(0)
    t = ATT_TILE
    group = DIFF_HEADS // DIFF_KV_HEADS
    rows = 2 * group * t
    lam = _diff_lambda(lp_ref[...], lambda_init)
    for kv in range(DIFF_KV_HEADS):
        q = q_ref[kv].reshape(rows, LANES)
        cols = slice(kv * LANES, (kv + 1) * LANES)
        _flash_init(m_ref, l_ref, acc_ref)

        def logits(j, w, q=q, cols=cols):
            return _dot_nt(k_ref[pl.ds(pl.multiple_of(j * t, t), w * t), cols], q)

        def update(st, j, w, cols=cols):
            _flash_update_t(st, _vt_tiles(vt_ref, j, w, cols), m_ref, l_ref, acc_ref)

        _far_tiles(jnp.maximum(i - 1, 0), logits, update, st0_ref, st1_ref)

        @pl.when(i >= 1)
        def _(logits=logits, update=update, kv=kv):
            update(logits(i - 1, 1) + bias_ref[kv, 1], i - 1, 1)

        update(logits(i, 1) + bias_ref[kv, 0], i, 1)
        ot = acc_ref[...] / l_ref[...]
        o = (ot[:, :group * t] - lam * ot[:, group * t:]).T
        for g in range(group):
            hd = kv * group + g
            o_ref[:, hd * LANES:(hd + 1) * LANES] = o[g * t:(g + 1) * t]


def _diff_prompt(dq, dkb, dvt, bias, lp, lambda_init):
    s = dkb.shape[0]
    t = ATT_TILE
    rows = 4 * t
    return pl.pallas_call(
        functools.partial(_diff_prompt_kernel, lambda_init=lambda_init),
        out_shape=jax.ShapeDtypeStruct((s, DIFF_WIDTH), F32),
        grid=(s // t,),
        in_specs=[pl.BlockSpec((DIFF_KV_HEADS, 4, t, LANES), lambda i: (0, 0, i, 0)),
                  _resident(dkb.shape), _resident(dvt.shape), _resident(bias.shape), _full(lp.shape)],
        out_specs=pl.BlockSpec((t, DIFF_WIDTH), lambda i: (i, 0)),
        scratch_shapes=[pltpu.VMEM((1, rows), F32), pltpu.VMEM((1, rows), F32),
                        pltpu.VMEM((LANES, rows), F32),
                        pltpu.VMEM((2 * t, rows), F32), pltpu.VMEM((2 * t, rows), F32)],
        compiler_params=_params(1),
        name="diff_prompt",
    )(dq, dkb, dvt, bias, lp)


_O_Q = 0
_O_K = _O_Q + MOBA_WIDTH
_O_V = _O_K + MOBA_KV_WIDTH
_O_G = _O_V + MOBA_KV_WIDTH
_O_END = _O_G + MOBA_WIDTH


def _mid_kernel(x_ref, olat_ref, wuv_ref, diffo_ref, gm_ref, gd_ref, subg_ref, wout_ref, ng_ref, win_ref,
                x1_ref, q_ref, k_ref, kb_ref, v_ref, vt_ref, g_ref, mean_ref, *, sub_scale):
    mla = jnp.concatenate([_dot(olat_ref[hd].astype(BF16), wuv_ref[hd]) for hd in range(MLA_HEADS)], axis=-1)
    dn = jnp.concatenate(
        [_rms(diffo_ref[:, hd * LANES:(hd + 1) * LANES], subg_ref[...], SUBLN_EPS) * sub_scale
         for hd in range(DIFF_HEADS)], axis=-1)
    mix = jnp.concatenate([mla * _silu(gm_ref[...]), dn * _silu(gd_ref[...])], axis=-1).astype(BF16)
    x1 = x_ref[...] + _dot(mix, wout_ref[...])
    x1_ref[...] = x1
    h = _rms(x1, ng_ref[...], NORM_EPS).astype(BF16)
    z = _dot(h, win_ref[...])
    q_ref[...] = z[:, _O_Q:_O_K]
    k = z[:, _O_K:_O_V]
    v = z[:, _O_V:_O_G]
    k_ref[...] = k
    hot = lax.broadcasted_iota(jnp.int32, (k.shape[0], LANES), 1) == pl.program_id(0)
    hot = jnp.where(hot, 1.0, 0.0)
    kb_ref[...] = jnp.concatenate(
        [piece for kv in range(MOBA_KV_HEADS) for piece in (k[:, kv * LANES:(kv + 1) * LANES], hot)],
        axis=-1).astype(BF16)
    v_ref[...] = v
    vt_ref[0] = v.T.astype(BF16)
    g_ref[...] = z[:, _O_G:_O_END]
    mean_ref[0] = jnp.mean(k, axis=0, keepdims=True)


def _mid(x, olat, wuv, diffo, gm, gd, subg, wout, ng, win, sub_scale):
    n, d = x.shape
    tm = min(ROW_TILE, n)
    row = lambda w: pl.BlockSpec((tm, w), lambda i: (i, 0))
    out_shape = (
        jax.ShapeDtypeStruct((n, d), F32), jax.ShapeDtypeStruct((n, MOBA_WIDTH), F32),
        jax.ShapeDtypeStruct((n, MOBA_KV_WIDTH), F32), jax.ShapeDtypeStruct((n, 2 * MOBA_KV_WIDTH), BF16),
        jax.ShapeDtypeStruct((n, MOBA_KV_WIDTH), F32), jax.ShapeDtypeStruct((n // tm, MOBA_KV_WIDTH, tm), BF16),
        jax.ShapeDtypeStruct((n, MOBA_WIDTH), F32), jax.ShapeDtypeStruct((n // tm, 1, MOBA_KV_WIDTH), F32),
    )
    out_specs = (row(d), row(MOBA_WIDTH), row(MOBA_KV_WIDTH), row(2 * MOBA_KV_WIDTH), row(MOBA_KV_WIDTH),
                 pl.BlockSpec((1, MOBA_KV_WIDTH, tm), lambda i: (i, 0, 0)), row(MOBA_WIDTH),
                 pl.BlockSpec((1, 1, MOBA_KV_WIDTH), lambda i: (i, 0, 0)))
    return pl.pallas_call(
        functools.partial(_mid_kernel, sub_scale=sub_scale),
        out_shape=out_shape,
        grid=(n // tm,),
        in_specs=[row(d), pl.BlockSpec((MLA_HEADS, tm, MLA_KV_LORA), lambda i: (0, i, 0)), _full(wuv.shape),
                  row(DIFF_WIDTH), row(MLA_WIDTH), row(DIFF_WIDTH), _full(subg.shape), _full(wout.shape),
                  _full(ng.shape), _full(win.shape)],
        out_specs=out_specs,
        compiler_params=_params(1),
        name="even_out_odd_in",
    )(x, olat, wuv, diffo, gm, gd, subg, wout, ng, win)


def _topk_mask(gate, ksel):
    lane = lax.broadcasted_iota(jnp.int32, gate.shape, 1).astype(F32)
    sel = jnp.zeros(gate.shape, jnp.bool_)
    for _ in range(ksel):
        mx = jnp.max(gate, axis=-1, keepdims=True)
        idx = jnp.min(jnp.where(gate == mx, lane, float(gate.shape[1])), axis=-1, keepdims=True)
        pick = lane == idx
        sel = jnp.logical_or(sel, pick)
        gate = jnp.where(pick, -jnp.inf, gate)
    return sel


def _moba_prompt_kernel(q_ref, mean_ref, k_ref, vt_ref, bias_ref, o_ref, m_ref, l_ref, acc_ref, st0_ref, st1_ref, *,
                        nblocks):
    i = pl.program_id(0)
    t = ATT_TILE
    nbp = mean_ref.shape[0]
    group = MOBA_HEADS // MOBA_KV_HEADS
    rows = group * t
    scale = MOBA_HEAD_DIM ** -0.5 * LOG2E
    blk = lax.broadcasted_iota(jnp.int32, (rows, nbp), 1)
    for kv in range(MOBA_KV_HEADS):
        cols = slice(kv * LANES, (kv + 1) * LANES)
        qf = jnp.concatenate(
            [q_ref[:, (kv * group + g) * LANES:(kv * group + g + 1) * LANES] for g in range(group)], axis=0)
        gate = lax.dot_general(qf, mean_ref[:, cols], (((1,), (1,)), ((), ())),
                               precision=lax.Precision.HIGHEST, preferred_element_type=F32)
        past = blk < i
        sel = jnp.logical_and(_topk_mask(jnp.where(past, gate, NEG_INF), min(MOBA_TOPK, nblocks)), past)
        q = jnp.concatenate([qf * scale, jnp.where(sel, 0.0, NEG_INF)], axis=-1).astype(BF16)
        _flash_init(m_ref, l_ref, acc_ref)

        def logits(j, w, masked=True, q=q, kv=kv):
            rows_j = pl.ds(pl.multiple_of(j * t, t), w * t)
            k0 = kv * (LANES + nbp)
            if masked:
                return _dot_nt(k_ref[rows_j, k0:k0 + LANES + nbp], q)
            return _dot_nt(k_ref[rows_j, k0:k0 + LANES], q[:, :LANES])

        def update(st, j, w, cols=cols):
            _flash_update_t(st, _vt_tiles(vt_ref, j, w, cols), m_ref, l_ref, acc_ref)

        _far_tiles(jnp.maximum(i - 1, 0), logits, update, st0_ref, st1_ref)

        @pl.when(i >= 1)
        def _(logits=logits, update=update, kv=kv):
            update(logits(i - 1, 1) + bias_ref[kv, 1], i - 1, 1)

        update(logits(i, 1, masked=False) + bias_ref[kv, 0], i, 1)
        o = (acc_ref[...] / l_ref[...]).T
        for g in range(group):
            hd = kv * group + g
            o_ref[:, hd * LANES:(hd + 1) * LANES] = o[g * t:(g + 1) * t]


def _moba_prompt(q, means, kb, vt, bias, nblocks):
    s = kb.shape[0]
    t = ATT_TILE
    rows = (MOBA_HEADS // MOBA_KV_HEADS) * t
    return pl.pallas_call(
        functools.partial(_moba_prompt_kernel, nblocks=nblocks),
        out_shape=jax.ShapeDtypeStruct((s, MOBA_WIDTH), F32),
        grid=(s // t,),
        in_specs=[pl.BlockSpec((t, MOBA_WIDTH), lambda i: (i, 0)), _resident(means.shape),
                  _resident(kb.shape), _resident(vt.shape), _resident(bias.shape)],
        out_specs=pl.BlockSpec((t, MOBA_WIDTH), lambda i: (i, 0)),
        scratch_shapes=[pltpu.VMEM((1, rows), F32), pltpu.VMEM((1, rows), F32),
                        pltpu.VMEM((LANES, rows), F32),
                        pltpu.VMEM((2 * t, rows), F32), pltpu.VMEM((2 * t, rows), F32)],
        compiler_params=_params(1),
        name="moba_prompt",
    )(q, means, kb, vt, bias)


def _final_kernel(x_ref, o_ref, g_ref, wout_ref, fg_ref, y_ref):
    mix = (o_ref[...] * _silu(g_ref[...])).astype(BF16)
    x2 = x_ref[...] + _dot(mix, wout_ref[...])
    y_ref[...] = _rms(x2, fg_ref[...], NORM_EPS)


def _final(x1, o, g, wout, fg):
    n, d = x1.shape
    tm = min(ROW_TILE, n)
    row = lambda w: pl.BlockSpec((tm, w), lambda i: (i, 0))
    return pl.pallas_call(
        _final_kernel,
        out_shape=jax.ShapeDtypeStruct((n, d), F32),
        grid=(n // tm,),
        in_specs=[row(d), row(MOBA_WIDTH), row(MOBA_WIDTH), _full(wout.shape), _full(fg.shape)],
        out_specs=row(d),
        compiler_params=_params(1),
        name="odd_out_final",
    )(x1, o, g, wout, fg)


PAGE_RING = 3


def _fetch_pages(pt_ref, caches, bufs, sems, n):
    b, c = pl.program_id(0), pl.program_id(1)
    steps = pl.num_programs(1)
    total = pl.num_programs(0) * steps
    step = b * steps + c
    ahead = PAGE_RING - 1

    def copies(flat, lookup):
        sslot = lax.rem(flat, PAGE_RING)
        sb, sc = lax.div(flat, steps), lax.rem(flat, steps)
        for p in range(n):
            page = pt_ref[sb, sc * n + p] if lookup else 0
            for cache, buf, sem in zip(caches, bufs, sems):
                yield pltpu.make_async_copy(cache.at[page], buf.at[sslot, p], sem.at[sslot])

    def start(flat):
        @pl.when(flat < total)
        def _():
            for cp in copies(flat, True):
                cp.start()

    @pl.when(step == 0)
    def _():
        for k in range(ahead):
            start(step + k)

    start(step + ahead)
    for cp in copies(step, False):
        cp.wait()
    return lax.rem(step, PAGE_RING)


def _page_scratch(caches, n):
    bufs = [pltpu.VMEM((PAGE_RING, n) + c.shape[1:], c.dtype) for c in caches]
    return bufs + [pltpu.SemaphoreType.DMA((PAGE_RING,)) for _ in caches]


def _hbm_specs(caches):
    return [pl.BlockSpec(memory_space=pl.ANY) for _ in caches]


def _seq_spec(shape):
    return pl.BlockSpec((1,) + shape, lambda b, c, pt: (b,) + (0,) * len(shape))


def _const_spec(a):
    return pl.BlockSpec(a.shape, lambda b, c, pt: (0,) * a.ndim)


def _mla_sample_kernel(pt_ref, q_ref, knew_ref, mask_ref, ckv_hbm, krt_hbm, o_ref, m_ref, l_ref, acc_ref,
                       ckv_buf, krt_buf, ckv_sem, krt_sem, *, n):
    slot = _fetch_pages(pt_ref, (ckv_hbm, krt_hbm), (ckv_buf, krt_buf), (ckv_sem, krt_sem), n)
    c = pl.program_id(1)
    q = q_ref[0]

    @pl.when(c == 0)
    def _():
        _flash_init(m_ref, l_ref, acc_ref)

    ckv = ckv_buf[slot].reshape(n * PAGE_SIZE, MLA_KV_LORA).astype(BF16)
    krt = jnp.concatenate([krt_buf[slot, p].astype(BF16) for p in range(n)], axis=1)
    s = _dot_nt(q[:, :MLA_KV_LORA], ckv) + _dot(q[:, MLA_KV_LORA:MLA_KV_LORA + MLA_ROPE], krt)
    _flash_update(s, ckv, m_ref, l_ref, acc_ref)

    @pl.when(c == pl.num_programs(1) - 1)
    def _():
        knew = knew_ref[0]
        _flash_update(_dot_nt(q, knew) + mask_ref[...], knew[:, :MLA_KV_LORA], m_ref, l_ref, acc_ref)
        o_ref[0] = acc_ref[...] / l_ref[...]


def _mla_sample(page_table, q, knew, mask, cache_ckv, cache_krt):
    nseq, rows, _ = q.shape
    n = min(MLA_PAGES_PER_STEP, page_table.shape[1])
    assert page_table.shape[1] % n == 0
    steps = page_table.shape[1] // n
    caches = (cache_ckv, cache_krt)
    grid_spec = pltpu.PrefetchScalarGridSpec(
        num_scalar_prefetch=1,
        grid=(nseq, steps),
        in_specs=[_seq_spec((rows, 2 * LANES)), _seq_spec((LANES, 2 * LANES)), _const_spec(mask)]
        + _hbm_specs(caches),
        out_specs=_seq_spec((rows, MLA_KV_LORA)),
        scratch_shapes=[pltpu.VMEM((rows, 1), F32), pltpu.VMEM((rows, 1), F32),
                        pltpu.VMEM((rows, MLA_KV_LORA), F32)] + _page_scratch(caches, n),
    )
    return pl.pallas_call(
        functools.partial(_mla_sample_kernel, n=n),
        out_shape=jax.ShapeDtypeStruct((nseq, rows, MLA_KV_LORA), F32),
        grid_spec=grid_spec,
        compiler_params=_params(2),
        name="mla_sample",
    )(page_table, q, knew, mask, *caches)


def _diff_sample_kernel(pt_ref, q_ref, knew_ref, vnew_ref, pmask_ref, blast_ref, bnew_ref, lp_ref, k_hbm, v_hbm,
                        o_ref, m_ref, l_ref, acc_ref, k_buf, v_buf, k_sem, v_sem, *, n, lambda_init):
    slot = _fetch_pages(pt_ref, (k_hbm, v_hbm), (k_buf, v_buf), (k_sem, v_sem), n)
    c = pl.program_id(1)
    last = pl.num_programs(1) - 1
    q = q_ref[0]
    step_rows = n * k_buf.shape[2]

    @pl.when(c == 0)
    def _():
        _flash_init(m_ref, l_ref, acc_ref)

    def pages(bias_ref):
        k = k_buf[slot].reshape(step_rows, LANES).astype(BF16)
        v = v_buf[slot].reshape(step_rows, LANES).astype(BF16)
        _flash_update(_dot_nt(q, k) + bias_ref[...], v, m_ref, l_ref, acc_ref)

    @pl.when(c != last)
    def _():
        pages(pmask_ref)

    @pl.when(c == last)
    def _():
        pages(blast_ref)
        _flash_update(_dot_nt(q, knew_ref[0].astype(BF16)) + bnew_ref[...], vnew_ref[0].astype(BF16),
                      m_ref, l_ref, acc_ref)
        lam = _diff_lambda(lp_ref[...], lambda_init)
        o = acc_ref[...] / l_ref[...]
        per_kv = q.shape[0] // DIFF_KV_HEADS
        half = per_kv // 2
        for kv in range(DIFF_KV_HEADS):
            r0 = kv * per_kv
            o_ref[0, kv] = o[r0:r0 + half] - lam * o[r0 + half:r0 + per_kv]


def _diff_sample(page_table, q, knew, vnew, pmask, blast, bnew, lp, cache_k, cache_v, lambda_init):
    nseq, rows, _ = q.shape
    n = min(PAGES_PER_STEP, page_table.shape[1])
    steps = page_table.shape[1] // n
    prow = cache_k.shape[1]
    caches = (cache_k, cache_v)
    grid_spec = pltpu.PrefetchScalarGridSpec(
        num_scalar_prefetch=1,
        grid=(nseq, steps),
        in_specs=[_seq_spec((rows, LANES)), _seq_spec((prow, LANES)), _seq_spec((prow, LANES)),
                  _const_spec(pmask), _const_spec(blast), _const_spec(bnew), _const_spec(lp)]
        + _hbm_specs(caches),
        out_specs=_seq_spec((DIFF_KV_HEADS, rows // 4, LANES)),
        scratch_shapes=[pltpu.VMEM((rows, 1), F32), pltpu.VMEM((rows, 1), F32), pltpu.VMEM((rows, LANES), F32)]
        + _page_scratch(caches, n),
    )
    return pl.pallas_call(
        functools.partial(_diff_sample_kernel, n=n, lambda_init=lambda_init),
        out_shape=jax.ShapeDtypeStruct((nseq, DIFF_KV_HEADS, rows // 4, LANES), F32),
        grid_spec=grid_spec,
        compiler_params=_params(2),
        name="diff_sample",
    )(page_table, q, knew, vnew, pmask, blast, bnew, lp, *caches)


def _moba_sample_kernel(pt_ref, q_ref, knew_ref, vnew_ref, pmask_ref, blast_ref, bnew_ref, k_hbm, v_hbm,
                        o_ref, gate_ref, mb_ref, lb_ref, ob_ref, k_buf, v_buf, k_sem, v_sem, *, n, nblocks):
    slot = _fetch_pages(pt_ref, (k_hbm, v_hbm), (k_buf, v_buf), (k_sem, v_sem), n)
    c = pl.program_id(1)
    last = pl.num_programs(1) - 1
    ppb = MOBA_BLOCK // PAGE_SIZE
    bps = n // ppb
    prow = k_buf.shape[2]
    brow = ppb * prow
    rows = q_ref.shape[1]
    qf = q_ref[0]
    qb = (qf * (MOBA_HEAD_DIM ** -0.5 * LOG2E)).astype(BF16)
    lane = lax.broadcasted_iota(jnp.int32, (rows, nblocks), 1)
    row_kv = lax.broadcasted_iota(jnp.int32, (rows, LANES), 0) // (rows // MOBA_KV_HEADS)
    key_kv = lax.broadcasted_iota(jnp.int32, (brow, LANES), 0) % MOBA_KV_HEADS

    @pl.when(c == 0)
    def _():
        for ref in (gate_ref, mb_ref, lb_ref):
            ref[...] = jnp.zeros(ref.shape, F32)

    def put(ref, blk, col):
        ref[...] = jnp.where(lane == blk, col, ref[...])

    def blocks(last_bias_ref):
        kf = k_buf[slot].reshape(n * prow, LANES)
        vb = v_buf[slot].reshape(n * prow, LANES).astype(BF16)
        s_all = _dot_nt(qb, kf.astype(BF16))
        for bi in range(bps):
            blk = c * bps + bi
            kblk = kf[bi * brow:(bi + 1) * brow]
            kmean = jnp.zeros((rows, LANES), F32)
            for kv in range(MOBA_KV_HEADS):
                mean_kv = jnp.sum(jnp.where(key_kv == kv, kblk, 0.0), axis=0, keepdims=True) * (1.0 / MOBA_BLOCK)
                kmean = jnp.where(row_kv == kv, mean_kv, kmean)
            put(gate_ref, blk, jnp.sum(qf * kmean, axis=-1, keepdims=True))
            bias_ref = last_bias_ref if bi == bps - 1 else pmask_ref
            s = s_all[:, bi * brow:(bi + 1) * brow] + bias_ref[...]
            m = jnp.max(s, axis=-1, keepdims=True)
            p = jnp.exp2(s - m)
            put(mb_ref, blk, m)
            put(lb_ref, blk, jnp.sum(p, axis=-1, keepdims=True))
            ob_ref[blk] = _dot(p.astype(BF16), vb[bi * brow:(bi + 1) * brow])

    @pl.when(c != last)
    def _():
        blocks(pmask_ref)

    @pl.when(c == last)
    def _():
        blocks(blast_ref)
        s = _dot_nt(qb, knew_ref[0].astype(BF16)) + bnew_ref[...]
        m_new = jnp.max(s, axis=-1, keepdims=True)
        p = jnp.exp2(s - m_new)
        l_new = jnp.sum(p, axis=-1, keepdims=True)
        o_new = _dot(p.astype(BF16), vnew_ref[0].astype(BF16))
        sel = _topk_mask(gate_ref[...], min(MOBA_TOPK, nblocks))
        mb = mb_ref[...]
        m_all = jnp.maximum(jnp.max(jnp.where(sel, mb, -jnp.inf), axis=-1, keepdims=True), m_new)
        w = jnp.where(sel, jnp.exp2(mb - m_all), 0.0)
        w_new = jnp.exp2(m_new - m_all)
        den = jnp.sum(w * lb_ref[...], axis=-1, keepdims=True) + w_new * l_new
        num = w_new * o_new
        for blk in range(nblocks):
            num = num + w[:, blk:blk + 1] * ob_ref[blk]
        o_ref[0] = num / den


def _moba_sample(page_table, q, knew, vnew, pmask, blast, bnew, cache_k, cache_v):
    nseq, rows, _ = q.shape
    ppb = MOBA_BLOCK // PAGE_SIZE
    nblocks = page_table.shape[1] // ppb
    n = min(PAGES_PER_STEP, page_table.shape[1])
    steps = page_table.shape[1] // n
    prow = cache_k.shape[1]
    caches = (cache_k, cache_v)
    grid_spec = pltpu.PrefetchScalarGridSpec(
        num_scalar_prefetch=1,
        grid=(nseq, steps),
        in_specs=[_seq_spec((rows, LANES)), _seq_spec((prow, LANES)), _seq_spec((prow, LANES)),
                  _const_spec(pmask), _const_spec(blast), _const_spec(bnew)]
        + _hbm_specs(caches),
        out_specs=_seq_spec((rows, LANES)),
        scratch_shapes=[pltpu.VMEM((rows, nblocks), F32), pltpu.VMEM((rows, nblocks), F32),
                        pltpu.VMEM((rows, nblocks), F32), pltpu.VMEM((nblocks, rows, LANES), F32)]
        + _page_scratch(caches, n),
    )
    return pl.pallas_call(
        functools.partial(_moba_sample_kernel, n=n, nblocks=nblocks),
        out_shape=jax.ShapeDtypeStruct((nseq, rows, LANES), F32),
        grid_spec=grid_spec,
        compiler_params=_params(2),
        name="moba_sample",
    )(page_table, q, knew, vnew, pmask, blast, bnew, *caches)


def _rope_tables(pos):
    half = MLA_ROPE // 2
    inv = ROPE_BASE ** (-jnp.arange(half, dtype=F32) / half)
    ang = pos.astype(F32)[:, None] * inv
    pad = jnp.zeros((pos.shape[0], LANES - 2 * half), F32)
    cos = jnp.cos(ang)
    sin = jnp.sin(ang)
    return jnp.concatenate([cos, cos, pad], axis=-1), jnp.concatenate([sin, sin, pad], axis=-1)


def _lane_pad(w, width):
    return jnp.pad(w, ((0, 0), (0, width - w.shape[1])))


def _even_weights(w_in, w_uq, w_uk, w_uv):
    half = MLA_ROPE // 2
    sizes = [MLA_Q_LORA, MLA_KV_LORA, MLA_ROPE, MLA_WIDTH, DIFF_WIDTH, DIFF_KV_WIDTH, DIFF_KV_WIDTH]
    cq, ckv, kr, gm, dq, dk, dv, gd = jnp.split(w_in, np.cumsum(sizes), axis=1)
    ka = _lane_pad(kr, LANES)
    kb = _lane_pad(jnp.concatenate([-kr[:, half:], kr[:, :half]], axis=1), LANES)
    win = jnp.concatenate([gm, dq, dk, dv, gd, ckv, _lane_pad(cq, 2 * LANES), ka, kb], axis=1).astype(BF16)
    per_head = lambda w: jnp.pad(w, ((0, 0), (0, 0), (0, LANES - w.shape[2]))).reshape(w.shape[0], MLA_HEADS * LANES)
    nope = per_head(w_uq[:, :, :MLA_NOPE])
    ra = per_head(w_uq[:, :, MLA_NOPE:])
    rb = per_head(jnp.concatenate([-w_uq[:, :, MLA_NOPE + half:], w_uq[:, :, MLA_NOPE:MLA_NOPE + half]], axis=2))
    wuq = jnp.concatenate([nope, ra, rb], axis=1).astype(BF16)
    wukt = jnp.pad(jnp.transpose(w_uk, (1, 2, 0)), ((0, 0), (0, LANES - MLA_NOPE), (0, 0))).astype(BF16)
    wuv = jnp.transpose(w_uv, (1, 0, 2)).astype(BF16)
    return win, wuq, wukt, wuv


def _interleave_kv(tile):
    nkv, r, j = tile.shape
    own = jnp.eye(nkv, dtype=jnp.bool_)[:, None, None, :]
    return jnp.where(own, tile[..., None], NEG_INF).reshape(nkv * r, j * nkv)


def kernel(x_prompt, x_sample, cache_mla_ckv, cache_mla_krope, cache_diff_k, cache_diff_v, cache_moba_k, cache_moba_v, page_table, norm_g, final_norm_g, rel_bias, w_in_even, mla_q_norm_g, mla_w_uq, mla_kv_norm_g, mla_w_uk, mla_w_uv, diff_lambda, diff_subln_g, w_out_even, w_in_odd, w_out_odd):
    _, s_len, d_model = x_prompt.shape
    n_dec, t_len, _ = x_sample.shape
    n_pages = page_table.shape[1]
    past_len = n_pages * PAGE_SIZE
    n_smp = n_dec * t_len
    t = ATT_TILE
    n_step = min(PAGES_PER_STEP, n_pages)
    assert norm_g.shape[0] == 2 and x_prompt.shape[0] == 1
    assert ROW_TILE == MOBA_BLOCK == ATT_TILE
    assert s_len % t == 0 and n_smp % min(ROW_TILE, n_smp) == 0 and past_len % MOBA_BLOCK == 0
    assert n_pages % n_step == 0 and n_step % (MOBA_BLOCK // PAGE_SIZE) == 0 and t_len <= 8
    assert FAR_DIST <= PAGE_SIZE + 1 and FAR_DIST <= t + 1
    dgroup = DIFF_HEADS // DIFF_KV_HEADS
    mgroup = MOBA_HEADS // MOBA_KV_HEADS
    nblocks_p = s_len // MOBA_BLOCK
    assert nblocks_p <= LANES
    lambda_init = 0.8 - 0.6 * math.exp(-0.3 * 0)

    tab = jnp.pad(rel_bias.astype(F32), ((0, 0), (0, 8)))
    zero_col = rel_bias.shape[1]
    band0 = _toeplitz_bias(tab, t, t, 0, True, transposed=True)
    band1 = _toeplitz_bias(tab, t, t, t, False, transposed=True)
    s_last_page = _toeplitz_bias(tab, 8, PAGE_SIZE, PAGE_SIZE, False)
    s_last_block = _toeplitz_bias(tab, 8, MOBA_BLOCK, MOBA_BLOCK, False)
    s_new = _toeplitz_bias(tab, 8, LANES, 0, True)
    diff_cols = [[m * DIFF_HEADS + kv * dgroup + g for m in range(2) for g in range(dgroup)]
                 for kv in range(DIFF_KV_HEADS)]
    moba_cols = [[kv * mgroup + g for g in range(mgroup)] for kv in range(MOBA_KV_HEADS)]
    lanes_of = lambda tb, cols: jnp.stack([jnp.concatenate([tb[c] for c in cl], axis=1) for cl in cols])
    rows_of = lambda tb, cols: jnp.stack([jnp.concatenate([tb[c, :t_len] for c in cl], axis=0) for cl in cols])
    diff_band = jnp.stack([lanes_of(band0, diff_cols), lanes_of(band1, diff_cols)], axis=1)
    moba_band = jnp.stack([lanes_of(band0, moba_cols), lanes_of(band1, moba_cols)], axis=1)
    mla_mask = jnp.tile(band0[zero_col], (1, MLA_HEADS // MLA_HEAD_PASSES))
    mla_new_mask = jnp.tile(s_new[zero_col, :t_len], (MLA_HEADS, 1))
    drows = 2 * dgroup * t_len
    diff_pmask = _interleave_kv(jnp.zeros((DIFF_KV_HEADS, drows, n_step * PAGE_SIZE), F32))
    diff_last = jnp.concatenate([diff_pmask[:, :(n_step - 1) * 2 * PAGE_SIZE],
                                 _interleave_kv(rows_of(s_last_page, diff_cols))], axis=1)
    diff_new = _interleave_kv(rows_of(s_new, diff_cols))
    moba_pmask = _interleave_kv(jnp.zeros((MOBA_KV_HEADS, mgroup * t_len, MOBA_BLOCK), F32))
    moba_last = _interleave_kv(rows_of(s_last_block, moba_cols))
    moba_new = _interleave_kv(rows_of(s_new, moba_cols))

    win_e, wuq, wukt, wuv = _even_weights(w_in_even[0], mla_w_uq[0], mla_w_uk[0], mla_w_uv[0])
    wout_e = w_out_even[0].astype(BF16)
    win_o = w_in_odd[0].astype(BF16)
    wout_o = w_out_odd[0].astype(BF16)
    ng0, ng1, fg = norm_g[0][None], norm_g[1][None], final_norm_g[None]
    qg, kvg, subg = mla_q_norm_g[0][None], mla_kv_norm_g[0][None], diff_subln_g[0][None]
    lp = diff_lambda[0].astype(F32)

    xp = x_prompt[0]
    xs = x_sample.reshape(n_smp, d_model)
    cos_p, sin_p = _rope_tables(jnp.arange(s_len))
    cos_s, sin_s = _rope_tables(jnp.tile(past_len + jnp.arange(t_len), n_dec))

    krt_cache = jnp.swapaxes(cache_mla_krope, 2, 3)
    kv_rows = lambda a: a[0].reshape(a.shape[1], a.shape[2] * a.shape[3], a.shape[4])
    new_rows = lambda a, nkv: jnp.pad(a.reshape(n_dec, t_len * nkv, LANES),
                                      ((0, 0), (0, (PAGE_SIZE - t_len) * nkv), (0, 0)))

    (qm_p, kcat_p, ckvt_p, ckv_p, kr_p, gm_p, dq_p, dk_p, dkb_p, dv_p, dvt_p, gd_p) = _even_in(
        xp, ng0, win_e, qg, wuq, wukt, kvg, cos_p, sin_p)
    olat_p = _mla_prompt(qm_p, kcat_p, ckvt_p, mla_mask)
    diffo_p = _diff_prompt(dq_p, dkb_p, dvt_p, diff_band, lp, lambda_init)
    (x1_p, q_p, k_p, kb_p, v_p, vt_p, g_p, means_p) = _mid(
        xp, olat_p, wuv, diffo_p, gm_p, gd_p, subg, wout_e, ng1, win_o, 1.0 - lambda_init)

    (qm_s, kcat_s, _, ckv_s, kr_s, gm_s, dq_s, dk_s, _, dv_s, _, gd_s) = _even_in(
        xs, ng0, win_e, qg, wuq, wukt, kvg, cos_s, sin_s)
    q_mla = qm_s.reshape(MLA_HEADS, n_dec, t_len, 2 * LANES).transpose(1, 0, 2, 3).reshape(
        n_dec, MLA_HEADS * t_len, 2 * LANES)
    knew_mla = jnp.pad(kcat_s.reshape(n_dec, t_len, 2 * LANES), ((0, 0), (0, LANES - t_len), (0, 0)))
    olat_s = _mla_sample(page_table, q_mla, knew_mla, mla_new_mask, cache_mla_ckv[0], krt_cache[0])
    olat_s = olat_s.reshape(n_dec, MLA_HEADS, t_len, MLA_KV_LORA).transpose(1, 0, 2, 3).reshape(
        MLA_HEADS, n_smp, MLA_KV_LORA)
    q_diff = dq_s.reshape(DIFF_KV_HEADS, 4, n_dec, t_len, LANES).transpose(2, 0, 1, 3, 4).reshape(
        n_dec, DIFF_KV_HEADS * drows, LANES)
    diffo_s = _diff_sample(page_table, q_diff, new_rows(dk_s, DIFF_KV_HEADS), new_rows(dv_s, DIFF_KV_HEADS),
                           diff_pmask, diff_last, diff_new, lp, kv_rows(cache_diff_k), kv_rows(cache_diff_v),
                           lambda_init)
    diffo_s = diffo_s.reshape(n_dec, DIFF_KV_HEADS, dgroup, t_len, LANES).transpose(0, 3, 1, 2, 4).reshape(
        n_smp, DIFF_WIDTH)
    (x1_s, q_s, k_s, _, v_s, _, g_s, _) = _mid(
        xs, olat_s, wuv, diffo_s, gm_s, gd_s, subg, wout_e, ng1, win_o, 1.0 - lambda_init)

    means = means_p.reshape(nblocks_p, MOBA_KV_WIDTH)
    means = jnp.pad(means, ((0, -nblocks_p % LANES), (0, 0)))
    o_p = _moba_prompt(q_p, means, kb_p, vt_p, moba_band, nblocks_p)
    y_p = _final(x1_p, o_p, g_p, wout_o, fg)

    q_moba = q_s.reshape(n_dec, t_len, MOBA_KV_HEADS, mgroup, LANES).transpose(0, 2, 3, 1, 4).reshape(
        n_dec, MOBA_HEADS * t_len, LANES)
    o_s = _moba_sample(page_table, q_moba, new_rows(k_s, MOBA_KV_HEADS), new_rows(v_s, MOBA_KV_HEADS),
                       moba_pmask, moba_last, moba_new, kv_rows(cache_moba_k), kv_rows(cache_moba_v))
    o_s = o_s.reshape(n_dec, MOBA_KV_HEADS, mgroup, t_len, LANES).transpose(0, 3, 1, 2, 4).reshape(
        n_smp, MOBA_WIDTH)
    y_s = _final(x1_s, o_s, g_s, wout_o, fg)

    kv4 = lambda a, b, s: a.reshape(1, b, s, 2, LANES)
    return (y_p[None], y_s.reshape(n_dec, t_len, d_model),
            ckv_p.reshape(1, 1, s_len, MLA_KV_LORA), kr_p.reshape(1, 1, s_len, MLA_ROPE),
            kv4(dk_p, 1, s_len), kv4(dv_p, 1, s_len), kv4(k_p, 1, s_len), kv4(v_p, 1, s_len),
            ckv_s.reshape(1, n_dec, t_len, MLA_KV_LORA), kr_s.reshape(1, n_dec, t_len, MLA_ROPE),
            kv4(dk_s, n_dec, t_len), kv4(dv_s, n_dec, t_len), kv4(k_s, n_dec, t_len), kv4(v_s, n_dec, t_len))
```
